```python
import math
import jax, jax.numpy as jnp
from jax import lax
import numpy as np

D_MODEL = 4096
BATCH = 32
SEQ = 256
DEPTH = 4
DEC_BATCH = 4
DEC_SEQ = 2048
PAST_LEN = 512

GRID_W = 64
HEAD_DIM = 128
N_HEADS = 16
N_KV_HEADS = 4
GQA_GROUP = N_HEADS // N_KV_HEADS
ATTN_WIDTH = N_HEADS * HEAD_DIM
KV_WIDTH = N_KV_HEADS * HEAD_DIM
WINDOW = 128
BLOCK = 128
ROPE_BASE = 10000.0
AXIS_DIM = HEAD_DIM // 2
AXIS_PAIRS = AXIS_DIM // 2
CONV_WIDTH = D_MODEL // 4
HYENA_WIDTH = D_MODEL // 4
MIX_WIDTH = ATTN_WIDTH + CONV_WIDTH + HYENA_WIDTH
N_BRANCH = 3
IN_COLS = ATTN_WIDTH + 2 * KV_WIDTH + 3 * CONV_WIDTH + 3 * HYENA_WIDTH
FILTER_EMB = 33
FILTER_HIDDEN = 64
HYENA_TARGET = 1e-2
FAST_DECAY_PCT = 0.3
SLOW_DECAY_PCT = 1.5
D_FF = ((8 * D_MODEL // 3 + 255) // 256) * 256
EPS = 1e-6
SCALE = HEAD_DIM ** -0.5
NEG_INF = -1e30
F32 = jnp.float32

kernel_name = "hybrid_diffusion_gated_parallel_step"


def rms_norm(x, g):
    xf = x.astype(F32)
    y = xf * lax.rsqrt(jnp.mean(xf * xf, axis=-1, keepdims=True) + EPS)
    return (y * g.astype(F32)).astype(x.dtype)


def modulation(cvec, w_mod, b_mod):
    m = jax.nn.silu(cvec) @ w_mod + b_mod
    return jnp.split(m[..., None, :], 6, axis=-1)


def conv3(u, w):
    up = jnp.pad(u, ((0, 0), (1, 1), (0, 0)))
    return up[:, :-2] * w[0] + up[:, 1:-1] * w[1] + up[:, 2:] * w[2]


def axial_rope_tables(L):
    rows = L // GRID_W
    t_row = jnp.repeat(jnp.arange(rows, dtype=F32), GRID_W)
    t_col = jnp.tile(jnp.arange(GRID_W, dtype=F32), rows)
    inv = ROPE_BASE ** (-jnp.arange(0, AXIS_DIM, 2, dtype=F32) / AXIS_DIM)
    ang = jnp.concatenate([t_row[:, None] * inv, t_col[:, None] * inv], axis=-1)
    return jnp.cos(ang), jnp.sin(ang)


def apply_axial_rope(x, cos, sin):
    xf = x.astype(F32)
    c = cos[None, :, None, :]
    s = sin[None, :, None, :]

    def rot(u, cc, ss):
        u1, u2 = jnp.split(u, 2, axis=-1)
        return jnp.concatenate([u1 * cc - u2 * ss, u2 * cc + u1 * ss], axis=-1)

    xr, xc = jnp.split(xf, 2, axis=-1)
    out = jnp.concatenate([rot(xr, c[..., :AXIS_PAIRS], s[..., :AXIS_PAIRS]),
                           rot(xc, c[..., AXIS_PAIRS:], s[..., AXIS_PAIRS:])], axis=-1)
    return out.astype(x.dtype)


def sink_softmax(s, sink):
    sk = sink.astype(F32).reshape(N_KV_HEADS, GQA_GROUP, 1, 1)
    m = jnp.maximum(jnp.max(s, axis=-1, keepdims=True), sk)
    p = jnp.exp(s - m)
    return p / (jnp.sum(p, axis=-1, keepdims=True) + jnp.exp(sk - m))


def context_attention(q, k, v, sink):
    B, S = q.shape[:2]
    nb = S // BLOCK
    qb = q.reshape(B, nb, BLOCK, N_KV_HEADS, GQA_GROUP, HEAD_DIM).transpose(1, 0, 2, 3, 4, 5)

    def block(qblk):
        s = jnp.einsum('bqhgd,bkhd->bhgqk', qblk, k, preferred_element_type=F32) * SCALE
        p = sink_softmax(s, sink).astype(v.dtype)
        return jnp.einsum('bhgqk,bkhd->bqhgd', p, v)

    o = lax.map(block, qb)
    return o.transpose(1, 0, 2, 3, 4, 5).reshape(B, S, ATTN_WIDTH)


def latent_attention(q, k, v, ck, cv, sink):
    B, L = q.shape[:2]
    nb = L // BLOCK
    nw = BLOCK + 2 * WINDOW
    qb = q.reshape(B, nb, BLOCK, N_KV_HEADS, GQA_GROUP, HEAD_DIM).transpose(1, 0, 2, 3, 4, 5)
    kp = jnp.pad(k, ((0, 0), (WINDOW, WINDOW), (0, 0), (0, 0)))
    vp = jnp.pad(v, ((0, 0), (WINDOW, WINDOW), (0, 0), (0, 0)))

    def block(args):
        n, qblk = args
        start = n * BLOCK
        kw = lax.dynamic_slice_in_dim(kp, start, nw, axis=1)
        vw = lax.dynamic_slice_in_dim(vp, start, nw, axis=1)
        qpos = start + jnp.arange(BLOCK)
        kpos = start - WINDOW + jnp.arange(nw)
        valid = (jnp.abs(qpos[:, None] - kpos[None, :]) <= WINDOW) & (kpos >= 0)[None, :] & (kpos < L)[None, :]
        sw = jnp.einsum('bqhgd,bkhd->bhgqk', qblk, kw, preferred_element_type=F32) * SCALE
        sw = jnp.where(valid, sw, NEG_INF)
        sc = jnp.einsum('bqhgd,bkhd->bhgqk', qblk, ck, preferred_element_type=F32) * SCALE
        p = sink_softmax(jnp.concatenate([sw, sc], axis=-1), sink).astype(v.dtype)
        return (jnp.einsum('bhgqk,bkhd->bqhgd', p[..., :nw], vw)
                + jnp.einsum('bhgqk,bkhd->bqhgd', p[..., nw:], cv))

    o = lax.map(block, (jnp.arange(nb), qb))
    return o.transpose(1, 0, 2, 3, 4, 5).reshape(B, L, ATTN_WIDTH)


def short_conv_mixer(u, conv_w):
    b_gate, c_gate, xin = jnp.split(u, 3, axis=-1)
    return b_gate * conv3(c_gate * xin, conv_w)


def hyena_filters(L, w1, b1, freq, w2, b2, w3):
    pos = jnp.arange(L, dtype=F32)
    t = jnp.linspace(0.0, 1.0, L, dtype=F32)
    bands = (FILTER_EMB - 1) // 2
    f = jnp.linspace(1e-4, bands - 1, bands, dtype=F32)
    ang = 2.0 * math.pi * pos[:, None] * f[None, :] / L
    z = jnp.concatenate([t[:, None], jnp.cos(ang), -jnp.sin(ang)], axis=-1)
    fr = freq.astype(F32)
    hdn = jnp.sin(fr * (z @ w1.astype(F32) + b1.astype(F32)))
    hdn = jnp.sin(fr * (hdn @ w2.astype(F32) + b2.astype(F32)))
    hf = (hdn @ w3.astype(F32)).reshape(L, 2, HYENA_WIDTH)
    deltas = jnp.abs(jnp.linspace(math.log(HYENA_TARGET) / SLOW_DECAY_PCT,
                                  math.log(HYENA_TARGET) / FAST_DECAY_PCT, HYENA_WIDTH, dtype=F32))
    decay = jnp.exp(-t[:, None] * deltas[None, :])
    hf = hf * decay[:, None, :]
    return hf[:, 0], hf[:, 1]


def bidir_long_conv(u, h_fwd, h_bwd):
    L = u.shape[1]
    kern = jnp.concatenate([h_fwd, jnp.zeros_like(h_fwd[:1]), h_bwd[:0:-1]], axis=0)
    kf = jnp.fft.rfft(kern, n=2 * L, axis=0)
    uf = jnp.fft.rfft(u.astype(F32), n=2 * L, axis=1)
    return jnp.fft.irfft(uf * kf[None], n=2 * L, axis=1)[:, :L]


def hyena_mixer(u, conv_w, w1, b1, freq, w2, b2, w3, skip):
    z = conv3(u, conv_w)
    x0, x1, v = jnp.split(z, 3, axis=-1)
    g = (x1 * v).astype(F32)
    h_fwd, h_bwd = hyena_filters(u.shape[1], w1, b1, freq, w2, b2, w3)
    y = bidir_long_conv(g, h_fwd, h_bwd) + skip.astype(F32) * g
    return (x0.astype(F32) * y).astype(u.dtype)


def token_mixers(h, lp, rope, ctx_k, ctx_v):
    B, L, _ = h.shape
    proj = h @ lp['w_in']
    s1 = ATTN_WIDTH
    s2 = s1 + KV_WIDTH
    s3 = s2 + KV_WIDTH
    s4 = s3 + 3 * CONV_WIDTH
    q, k, v, conv_in, hyena_in = jnp.split(proj, [s1, s2, s3, s4], axis=-1)
    q = q.reshape(B, L, N_HEADS, HEAD_DIM)
    k = k.reshape(B, L, N_KV_HEADS, HEAD_DIM)
    v = v.reshape(B, L, N_KV_HEADS, HEAD_DIM)
    if rope is None:
        o_attn = context_attention(q, k, v, lp['sink'])
    else:
        cos, sin = rope
        qr = apply_axial_rope(q, cos, sin)
        kr = apply_axial_rope(k, cos, sin)
        o_attn = latent_attention(qr, kr, v, ctx_k, ctx_v, lp['sink'])
    o_conv = short_conv_mixer(conv_in, lp['conv_w'])
    o_hy = hyena_mixer(hyena_in, lp['hyena_conv_w'], lp['filt_w1'], lp['filt_b1'], lp['filt_freq'],
                       lp['filt_w2'], lp['filt_b2'], lp['filt_w3'], lp['hyena_skip'])
    wb = lp['w_branch']
    br_attn = o_attn @ wb[:ATTN_WIDTH]
    br_conv = o_conv @ wb[ATTN_WIDTH:ATTN_WIDTH + CONV_WIDTH]
    br_hy = o_hy @ wb[ATTN_WIDTH + CONV_WIDTH:]
    g = jax.nn.sigmoid(h @ lp['w_gate'] + lp['b_gate']).reshape(B, L, N_BRANCH, D_MODEL)
    merged = g[:, :, 0] * br_attn + g[:, :, 1] * br_conv + g[:, :, 2] * br_hy
    return merged @ lp['w_o'], k, v


def trunk_layer(x, mods, lp, rope, ctx_k, ctx_v):
    sh1, sc1, g1, sh2, sc2, g2 = mods
    h = rms_norm(x, lp['norm_g'][0]) * (1.0 + sc1) + sh1
    mix, k, v = token_mixers(h, lp, rope, ctx_k, ctx_v)
    x = x + g1 * rms_norm(mix, lp['norm_g'][1])
    h = rms_norm(x, lp['norm_g'][2]) * (1.0 + sc2) + sh2
    a, b = jnp.split(h @ lp['w_ffn_in'], 2, axis=-1)
    f = (jax.nn.silu(a) * b) @ lp['w_ffn_out']
    x = x + g2 * rms_norm(f, lp['norm_g'][3])
    return x, k, v


def setup_inputs(seed: int = 0) -> dict:
    key = jax.random.key(seed)
    ks = jax.random.split(key, 32)
    D = D_MODEL

    def nrm(k, shape, scale):
        return jax.random.normal(k, shape, dtype=F32) * scale

    return {
        'x_prompt': nrm(ks[0], (BATCH, SEQ, D), 1.0),
        'x_sample': nrm(ks[1], (DEC_BATCH, DEC_SEQ, D), 1.0),
        'cache_k': nrm(ks[2], (DEC_BATCH, DEPTH, PAST_LEN, N_KV_HEADS, HEAD_DIM), 1.0),
        'cache_v': nrm(ks[3], (DEC_BATCH, DEPTH, PAST_LEN, N_KV_HEADS, HEAD_DIM), 1.0),
        'c': nrm(ks[4], (DEC_BATCH, D), 1.0),
        'c_ctx': nrm(ks[5], (D,), 1.0),
        'norm_g': 1.0 + nrm(ks[6], (DEPTH, 4, D), 0.01),
        'w_mod': nrm(ks[7], (DEPTH, D, 6 * D), 0.5 * D ** -0.5),
        'b_mod': nrm(ks[8], (DEPTH, 6 * D), 0.01),
        'w_in': nrm(ks[9], (DEPTH, D, IN_COLS), D ** -0.5),
        'attn_sink': nrm(ks[10], (DEPTH, N_HEADS), 1.0),
        'conv_w': nrm(ks[11], (DEPTH, 3, CONV_WIDTH), 3 ** -0.5),
        'hyena_conv_w': nrm(ks[12], (DEPTH, 3, 3 * HYENA_WIDTH), 3 ** -0.5),
        'filt_w1': nrm(ks[13], (DEPTH, FILTER_EMB, FILTER_HIDDEN), FILTER_EMB ** -0.5),
        'filt_b1': nrm(ks[14], (DEPTH, FILTER_HIDDEN), 0.01),
        'filt_freq': 1.0 + nrm(ks[15], (DEPTH, FILTER_HIDDEN), 0.1),
        'filt_w2': nrm(ks[16], (DEPTH, FILTER_HIDDEN, FILTER_HIDDEN), FILTER_HIDDEN ** -0.5),
        'filt_b2': nrm(ks[17], (DEPTH, FILTER_HIDDEN), 0.01),
        'filt_w3': nrm(ks[18], (DEPTH, FILTER_HIDDEN, 2 * HYENA_WIDTH), FILTER_HIDDEN ** -0.5),
        'hyena_skip': nrm(ks[19], (DEPTH, HYENA_WIDTH), 0.1),
        'w_branch': nrm(ks[20], (DEPTH, MIX_WIDTH, D), MIX_WIDTH ** -0.5),
        'w_gate': nrm(ks[21], (DEPTH, D, N_BRANCH * D), D ** -0.5),
        'b_gate': nrm(ks[22], (DEPTH, N_BRANCH * D), 0.01),
        'w_o': nrm(ks[23], (DEPTH, D, D), D ** -0.5),
        'w_ffn_in': nrm(ks[24], (DEPTH, D, 2 * D_FF), D ** -0.5),
        'w_ffn_out': nrm(ks[25], (DEPTH, D_FF, D), D_FF ** -0.5),
    }


def reference(x_prompt, x_sample, cache_k, cache_v, c, c_ctx, norm_g, w_mod, b_mod, w_in, attn_sink,
              conv_w, hyena_conv_w, filt_w1, filt_b1, filt_freq, filt_w2, filt_b2, filt_w3, hyena_skip,
              w_branch, w_gate, b_gate, w_o, w_ffn_in, w_ffn_out):
    rope = axial_rope_tables(x_sample.shape[1])
    yp = x_prompt
    ys = x_sample
    ks_out = []
    vs_out = []
    for l in range(DEPTH):
        lp = {
            'norm_g': norm_g[l], 'w_in': w_in[l], 'sink': attn_sink[l], 'conv_w': conv_w[l],
            'hyena_conv_w': hyena_conv_w[l], 'filt_w1': filt_w1[l], 'filt_b1': filt_b1[l],
            'filt_freq': filt_freq[l], 'filt_w2': filt_w2[l], 'filt_b2': filt_b2[l], 'filt_w3': filt_w3[l],
            'hyena_skip': hyena_skip[l], 'w_branch': w_branch[l], 'w_gate': w_gate[l], 'b_gate': b_gate[l],
            'w_o': w_o[l], 'w_ffn_in': w_ffn_in[l], 'w_ffn_out': w_ffn_out[l],
        }
        mods_ctx = modulation(c_ctx, w_mod[l], b_mod[l])
        yp, k_l, v_l = trunk_layer(yp, mods_ctx, lp, None, None, None)
        ks_out.append(k_l)
        vs_out.append(v_l)
        mods_lat = modulation(c, w_mod[l], b_mod[l])
        ys, _, _ = trunk_layer(ys, mods_lat, lp, rope, cache_k[:, l], cache_v[:, l])
    new_k = jnp.stack(ks_out, axis=1)
    new_v = jnp.stack(vs_out, axis=1)
    return (yp, ys, new_k, new_v)
```

```python
import functools
import math

import jax
import jax.numpy as jnp
from jax import lax
from jax.experimental import pallas as pl
from jax.experimental.pallas import tpu as pltpu

F32 = jnp.float32
BF16 = jnp.bfloat16

HEAD_DIM = 128
N_HEADS = 16
N_KV_HEADS = 4
GQA_GROUP = N_HEADS // N_KV_HEADS
ATTN_WIDTH = N_HEADS * HEAD_DIM
KV_WIDTH = N_KV_HEADS * HEAD_DIM
WINDOW = 128
BLOCK = 128
GRID_W = 64
ROPE_BASE = 10000.0
AXIS_DIM = HEAD_DIM // 2
AXIS_PAIRS = AXIS_DIM // 2
FILTER_EMB = 33
HYENA_TARGET = 1e-2
FAST_DECAY_PCT = 0.3
SLOW_DECAY_PCT = 1.5
EPS = 1e-6
SCALE = HEAD_DIM ** -0.5
NEG_INF = -1e30
N_MOD = 6
MOD_ROWS = 8
FILT_PAD = 128

LANE = 128
MIB = 1024 * 1024
VMEM_CAP_BYTES = 56 * MIB
HI = lax.Precision.HIGHEST


def _cparams(semantics, vmem_bytes):
    return pltpu.CompilerParams(dimension_semantics=semantics,
                                vmem_limit_bytes=int(min(max(vmem_bytes, 16 * MIB), VMEM_CAP_BYTES)))


def _tile(n, pref):
    if n <= pref:
        return n
    t = (pref // LANE) * LANE
    while t > LANE and n % t:
        t -= LANE
    assert n % t == 0, (n, pref)
    return t


def _round_up(n, m):
    return -(-n // m) * m


def _mods_kernel(c_ref, w_ref, b_ref, o_ref):
    c = c_ref[...]
    a = (c * jax.nn.sigmoid(c)).astype(BF16)
    o_ref[0] = jnp.dot(a, w_ref[0].astype(BF16), preferred_element_type=F32) + b_ref[0]


def _modulations(cvecs, w_mod, b_mod):
    depth, d, n = w_mod.shape
    tn = _tile(n, 512)
    return pl.pallas_call(
        _mods_kernel,
        grid=(depth, n // tn),
        in_specs=[
            pl.BlockSpec((MOD_ROWS, d), lambda l, j: (0, 0)),
            pl.BlockSpec((1, d, tn), lambda l, j: (l, 0, j)),
            pl.BlockSpec((1, 1, tn), lambda l, j: (l, 0, j)),
        ],
        out_specs=pl.BlockSpec((1, MOD_ROWS, tn), lambda l, j: (l, 0, j)),
        out_shape=jax.ShapeDtypeStruct((depth, MOD_ROWS, n), F32),
        compiler_params=_cparams(("arbitrary", "arbitrary"), 2 * d * tn * 4 + 8 * MIB),
        name="modulations",
    )(cvecs, w_mod, b_mod.reshape(depth, 1, n))


def _rms(x):
    return x * lax.rsqrt(jnp.mean(x * x, axis=-1, keepdims=True) + EPS)


def _resnorm_kernel(*refs, has_y, has_h):
    it = iter(refs)
    x_ref = next(it)
    if has_y:
        y_ref, gny_ref, gate_ref = next(it), next(it), next(it)
    if has_h:
        gnx_ref, sc_ref, sh_ref = next(it), next(it), next(it)
    if has_y:
        xo_ref = next(it)
    if has_h:
        h_ref = next(it)
    x = x_ref[...]
    if has_y:
        x = x + gate_ref[0] * (_rms(y_ref[...]) * gny_ref[0])
        xo_ref[...] = x
    if has_h:
        h_ref[...] = ((_rms(x) * gnx_ref[0]) * (1.0 + sc_ref[0]) + sh_ref[0]).astype(h_ref.dtype)


def _resnorm(x, y, norm_tab, mod_tab, seq_of_tile, tm, *, y_norm=None, gate=None, h_norm=None, h_scale=None,
             h_shift=None):
    m, d = x.shape
    has_y, has_h = y is not None, h_norm is not None
    row = pl.BlockSpec((tm, d), lambda i: (i, 0))

    def tab(idx):
        return pl.BlockSpec((1, 1, d), lambda i: (idx, 0, 0))

    def mod(layer_which):
        layer, which = layer_which
        return pl.BlockSpec((1, 1, d), lambda i: ((layer * MOD_ROWS + seq_of_tile(i)) * N_MOD + which, 0, 0))

    args, specs = [x], [row]
    if has_y:
        args += [y, norm_tab, mod_tab]
        specs += [row, tab(y_norm), mod(gate)]
    if has_h:
        args += [norm_tab, mod_tab, mod_tab]
        specs += [tab(h_norm), mod(h_scale), mod(h_shift)]
    out_shape, out_specs = [], []
    if has_y:
        out_shape.append(jax.ShapeDtypeStruct((m, d), F32))
        out_specs.append(row)
    if has_h:
        out_shape.append(jax.ShapeDtypeStruct((m, d), BF16))
        out_specs.append(row)
    outs = pl.pallas_call(
        functools.partial(_resnorm_kernel, has_y=has_y, has_h=has_h),
        grid=(m // tm,),
        in_specs=specs,
        out_specs=out_specs,
        out_shape=out_shape,
        compiler_params=_cparams(("arbitrary",), 2 * tm * d * 14 + 8 * MIB),
        name="resnorm",
    )(*args)
    return outs


def _mm_kernel(a_ref, w_ref, o_ref):
    o_ref[...] = jnp.dot(a_ref[...], w_ref[0], preferred_element_type=F32).astype(o_ref.dtype)


def _matmul(a, w, layer, out_dtype, tm, tn):
    m, k = a.shape
    n = w.shape[2]
    tm, tn = _tile(m, tm), _tile(n, tn)
    ob = jnp.dtype(out_dtype).itemsize
    return pl.pallas_call(
        _mm_kernel,
        grid=(m // tm, n // tn),
        in_specs=[pl.BlockSpec((tm, k), lambda i, j: (i, 0)),
                  pl.BlockSpec((1, k, tn), lambda i, j: (layer, 0, j))],
        out_specs=pl.BlockSpec((tm, tn), lambda i, j: (i, j)),
        out_shape=jax.ShapeDtypeStruct((m, n), out_dtype),
        compiler_params=_cparams(("arbitrary", "arbitrary"),
                                 2 * (tm * k * 2 + k * tn * 2 + tm * tn * ob) + 2 * tm * tn * 4 + 8 * MIB),
        name="matmul",
    )(a, w)


def _gate_kernel(a_ref, w_ref, b_ref, o_ref):
    acc = jnp.dot(a_ref[...], w_ref[0], preferred_element_type=F32) + b_ref[0]
    o_ref[...] = jax.nn.sigmoid(acc).astype(o_ref.dtype)


def _gates(h, w_gate, b_gate, layer, tm, tn):
    m, k = h.shape
    n = w_gate.shape[2]
    tm, tn = _tile(m, tm), _tile(n, tn)
    return pl.pallas_call(
        _gate_kernel,
        grid=(m // tm, n // tn),
        in_specs=[pl.BlockSpec((tm, k), lambda i, j: (i, 0)),
                  pl.BlockSpec((1, k, tn), lambda i, j: (layer, 0, j)),
                  pl.BlockSpec((1, 1, tn), lambda i, j: (layer, 0, j))],
        out_specs=pl.BlockSpec((tm, tn), lambda i, j: (i, j)),
        out_shape=jax.ShapeDtypeStruct((m, n), BF16),
        compiler_params=_cparams(("arbitrary", "arbitrary"),
                                 2 * (tm * k * 2 + k * tn * 2 + tm * tn * 2) + 2 * tm * tn * 4 + 8 * MIB),
        name="gates",
    )(h, w_gate, b_gate)


def _merge_kernel(oa_ref, oc_ref, oh_ref, wa_ref, wc_ref, wh_ref, ga_ref, gc_ref, gh_ref, o_ref):
    ba = jnp.dot(oa_ref[...], wa_ref[0], preferred_element_type=F32)
    bc = jnp.dot(oc_ref[...], wc_ref[0], preferred_element_type=F32)
    bh = jnp.dot(oh_ref[...], wh_ref[0], preferred_element_type=F32)
    merged = ga_ref[...].astype(F32) * ba + gc_ref[...].astype(F32) * bc + gh_ref[...].astype(F32) * bh
    o_ref[...] = merged.astype(o_ref.dtype)


def _merge(o_attn, o_conv, o_hy, w_branch, gates, layer, tm, tn):
    m = o_attn.shape[0]
    d = w_branch.shape[2]
    cw = o_conv.shape[1]
    assert ATTN_WIDTH % cw == 0
    tm, tn = _tile(m, tm), _tile(d, tn)
    nj = d // tn
    conv_blk = ATTN_WIDTH // cw
    return pl.pallas_call(
        _merge_kernel,
        grid=(m // tm, nj),
        in_specs=[pl.BlockSpec((tm, ATTN_WIDTH), lambda i, j: (i, 0)),
                  pl.BlockSpec((tm, cw), lambda i, j: (i, 0)),
                  pl.BlockSpec((tm, cw), lambda i, j: (i, 0)),
                  pl.BlockSpec((1, ATTN_WIDTH, tn), lambda i, j: (layer, 0, j)),
                  pl.BlockSpec((1, cw, tn), lambda i, j: (layer, conv_blk, j)),
                  pl.BlockSpec((1, cw, tn), lambda i, j: (layer, conv_blk + 1, j)),
                  pl.BlockSpec((tm, tn), lambda i, j: (i, j)),
                  pl.BlockSpec((tm, tn), lambda i, j: (i, nj + j)),
                  pl.BlockSpec((tm, tn), lambda i, j: (i, 2 * nj + j))],
        out_specs=pl.BlockSpec((tm, tn), lambda i, j: (i, j)),
        out_shape=jax.ShapeDtypeStruct((m, d), BF16),
        compiler_params=_cparams(("arbitrary", "arbitrary"),
                                 2 * (tm * (ATTN_WIDTH + 2 * cw) * 2 + (ATTN_WIDTH + 2 * cw) * tn * 2 + 4 * tm * tn * 2)
                                 + 3 * tm * tn * 4 + 8 * MIB),
        name="merge",
    )(o_attn, o_conv, o_hy, w_branch, w_branch, w_branch, gates, gates, gates)


def _ffn_in_kernel(h_ref, wa_ref, wb_ref, o_ref):
    a = jnp.dot(h_ref[...], wa_ref[0], preferred_element_type=F32)
    b = jnp.dot(h_ref[...], wb_ref[0], preferred_element_type=F32)
    o_ref[...] = (a * jax.nn.sigmoid(a) * b).astype(o_ref.dtype)


def _ffn_in(h, w_ffn_in, layer, tm, tn):
    m, k = h.shape
    ff = w_ffn_in.shape[2] // 2
    tm, tn = _tile(m, tm), _tile(ff, tn)
    nj = ff // tn
    return pl.pallas_call(
        _ffn_in_kernel,
        grid=(m // tm, nj),
        in_specs=[pl.BlockSpec((tm, k), lambda i, j: (i, 0)),
                  pl.BlockSpec((1, k, tn), lambda i, j: (layer, 0, j)),
                  pl.BlockSpec((1, k, tn), lambda i, j: (layer, 0, nj + j))],
        out_specs=pl.BlockSpec((tm, tn), lambda i, j: (i, j)),
        out_shape=jax.ShapeDtypeStruct((m, ff), BF16),
        compiler_params=_cparams(("arbitrary", "arbitrary"),
                                 2 * (tm * k * 2 + 2 * k * tn * 2 + tm * tn * 2) + 2 * tm * tn * 4 + 8 * MIB),
        name="ffn_in",
    )(h, w_ffn_in, w_ffn_in)


def _mm_kacc_kernel(a_ref, w_ref, o_ref):
    part = jnp.dot(a_ref[...], w_ref[0], preferred_element_type=F32)

    @pl.when(pl.program_id(2) == 0)
    def _():
        o_ref[...] = part

    @pl.when(pl.program_id(2) != 0)
    def _():
        o_ref[...] += part


def _matmul_ksplit(a, w, layer, tm, tn, n_k):
    m, k = a.shape
    n = w.shape[2]
    tm, tn = _tile(m, tm), _tile(n, tn)
    assert k % n_k == 0 and (k // n_k) % LANE == 0
    tk = k // n_k
    return pl.pallas_call(
        _mm_kacc_kernel,
        grid=(m // tm, n // tn, n_k),
        in_specs=[pl.BlockSpec((tm, tk), lambda i, j, kk: (i, kk)),
                  pl.BlockSpec((1, tk, tn), lambda i, j, kk: (layer, kk, j))],
        out_specs=pl.BlockSpec((tm, tn), lambda i, j, kk: (i, j)),
        out_shape=jax.ShapeDtypeStruct((m, n), F32),
        compiler_params=_cparams(("arbitrary", "arbitrary", "arbitrary"),
                                 2 * (tm * tk * 2 + tk * tn * 2 + tm * tn * 4) + tm * tn * 4 + 8 * MIB),
        name="matmul_ksplit",
    )(a, w)


def _softmax_pv(s, sink, v):
    m = jnp.maximum(jnp.max(s, axis=-1, keepdims=True), sink)
    p = jnp.exp(s - m)
    denom = jnp.sum(p, axis=-1, keepdims=True) + jnp.exp(sink - m)
    o = jnp.dot(p.astype(BF16), v, preferred_element_type=F32)
    return o / denom


def _ctx_attn_kernel(sink_ref, q_ref, k_ref, v_ref, o_ref, *, layer):
    for hk in range(N_KV_HEADS):
        cols = slice(hk * HEAD_DIM, (hk + 1) * HEAD_DIM)
        k = k_ref[:, cols].astype(BF16)
        v = v_ref[:, cols].astype(BF16)
        for g in range(GQA_GROUP):
            h = hk * GQA_GROUP + g
            hc = slice(h * HEAD_DIM, (h + 1) * HEAD_DIM)
            q = q_ref[:, hc].astype(BF16)
            s = lax.dot_general(q, k, (((1,), (1,)), ((), ())), preferred_element_type=F32) * SCALE
            o_ref[:, hc] = _softmax_pv(s, sink_ref[layer, h], v).astype(o_ref.dtype)


def _ctx_attention(proj, sink, layer, n_seq, seq):
    kb = ATTN_WIDTH // KV_WIDTH
    return pl.pallas_call(
        functools.partial(_ctx_attn_kernel, layer=layer),
        grid=(n_seq,),
        in_specs=[pl.BlockSpec(memory_space=pltpu.SMEM),
                  pl.BlockSpec((seq, ATTN_WIDTH), lambda b: (b, 0)),
                  pl.BlockSpec((seq, KV_WIDTH), lambda b: (b, kb)),
                  pl.BlockSpec((seq, KV_WIDTH), lambda b: (b, kb + 1))],
        out_specs=pl.BlockSpec((seq, ATTN_WIDTH), lambda b: (b, 0)),
        out_shape=jax.ShapeDtypeStruct((n_seq * seq, ATTN_WIDTH), BF16),
        compiler_params=_cparams(("arbitrary",), 32 * MIB),
        name="ctx_attention",
    )(sink, proj, proj, proj)


def _rope(x, cos, sin_signed, lo_half):
    partner = jnp.where(lo_half, pltpu.roll(x, HEAD_DIM - AXIS_PAIRS, 1), pltpu.roll(x, AXIS_PAIRS, 1))
    return x * cos + partner * sin_signed


def _lat_attn_kernel(sink_ref, q_ref, kp_ref, kc_ref, kn_ref, vp_ref, vc_ref, vn_ref, ck_ref, cv_ref,
                     cq_ref, sq_ref, cp_ref, sp_ref, cn_ref, sn_ref, o_ref, *, layer, n_blocks):
    n = pl.program_id(1)
    past = ck_ref.shape[2]
    nk = 3 * BLOCK + past
    lane = lax.broadcasted_iota(jnp.int32, (BLOCK, HEAD_DIM), 1)
    lo_half = (lane % AXIS_DIM) < AXIS_PAIRS
    qi = lax.broadcasted_iota(jnp.int32, (BLOCK, nk), 0)
    kj = lax.broadcasted_iota(jnp.int32, (BLOCK, nk), 1)
    first_ok = jnp.where(n > 0, qi, BLOCK)
    last_ok = jnp.where(n < n_blocks - 1, qi, -1)
    bias_prev = jnp.where(kj >= first_ok, 0.0, NEG_INF)
    bias_next = jnp.where(kj - 2 * BLOCK <= last_ok, 0.0, NEG_INF)
    bias = jnp.where(kj < BLOCK, bias_prev, jnp.where(kj < 2 * BLOCK, 0.0, jnp.where(kj < 3 * BLOCK, bias_next, 0.0)))
    cq, sq = cq_ref[...], sq_ref[...]
    for hk in range(N_KV_HEADS):
        cols = slice(hk * HEAD_DIM, (hk + 1) * HEAD_DIM)
        keys = jnp.concatenate([
            _rope(kp_ref[:, cols], cp_ref[...], sp_ref[...], lo_half).astype(BF16),
            _rope(kc_ref[:, cols], cq, sq, lo_half).astype(BF16),
            _rope(kn_ref[:, cols], cn_ref[...], sn_ref[...], lo_half).astype(BF16),
            ck_ref[0, 0, :, cols].astype(BF16)], axis=0)
        vals = jnp.concatenate([vp_ref[:, cols].astype(BF16), vc_ref[:, cols].astype(BF16),
                                vn_ref[:, cols].astype(BF16), cv_ref[0, 0, :, cols].astype(BF16)], axis=0)
        for g in range(GQA_GROUP):
            h = hk * GQA_GROUP + g
            hc = slice(h * HEAD_DIM, (h + 1) * HEAD_DIM)
            q = _rope(q_ref[:, hc], cq, sq, lo_half).astype(BF16)
            s = lax.dot_general(q, keys, (((1,), (1,)), ((), ())), preferred_element_type=F32) * SCALE + bias
            o_ref[:, hc] = _softmax_pv(s, sink_ref[layer, h], vals).astype(o_ref.dtype)


def _lat_attention(proj, cache_k, cache_v, sink, rope_cos, rope_sin, layer, row0, n_seq, seq):
    nb = seq // BLOCK
    kb = ATTN_WIDTH // KV_WIDTH
    b0 = row0 // BLOCK
    past = cache_k.shape[2]

    def blk(shift):
        return lambda b, n: (b0 + b * nb + jnp.clip(n + shift, 0, nb - 1))

    def rows(shift, col):
        f = blk(shift)
        return lambda b, n: (f(b, n), col)

    def tab(shift):
        return pl.BlockSpec((BLOCK, HEAD_DIM), lambda b, n: (jnp.clip(n + shift, 0, nb - 1), 0))

    kv = lambda shift, col: pl.BlockSpec((BLOCK, KV_WIDTH), rows(shift, col))
    cache = pl.BlockSpec((1, 1, past, KV_WIDTH), lambda b, n: (b, layer, 0, 0))
    return pl.pallas_call(
        functools.partial(_lat_attn_kernel, layer=layer, n_blocks=nb),
        grid=(n_seq, nb),
        in_specs=[pl.BlockSpec(memory_space=pltpu.SMEM),
                  pl.BlockSpec((BLOCK, ATTN_WIDTH), rows(0, 0)),
                  kv(-1, kb), kv(0, kb), kv(1, kb),
                  kv(-1, kb + 1), kv(0, kb + 1), kv(1, kb + 1),
                  cache, cache,
                  tab(0), tab(0), tab(-1), tab(-1), tab(1), tab(1)],
        out_specs=pl.BlockSpec((BLOCK, ATTN_WIDTH), lambda b, n: (b * nb + n, 0)),
        out_shape=jax.ShapeDtypeStruct((n_seq * seq, ATTN_WIDTH), BF16),
        compiler_params=_cparams(("arbitrary", "arbitrary"), 32 * MIB),
        name="latent_attention",
    )(sink, proj, proj, proj, proj, proj, proj, proj, cache_k, cache_v,
      rope_cos, rope_sin, rope_cos, rope_sin, rope_cos, rope_sin)


def _conv3(u, w):
    n_rows = u.shape[0]
    row = lax.broadcasted_iota(jnp.int32, u.shape, 0)
    before = jnp.where(row == 0, 0.0, pltpu.roll(u, 1, 0))
    after = jnp.where(row == n_rows - 1, 0.0, pltpu.roll(u, n_rows - 1, 0))
    return before * w[0:1] + u * w[1:2] + after * w[2:3]


def _short_conv_kernel(b_ref, c_ref, x_ref, w_ref, o_ref):
    o_ref[...] = (b_ref[...] * _conv3(c_ref[...] * x_ref[...], w_ref[0])).astype(o_ref.dtype)


def _short_conv(proj, conv_w, layer, col0, row0, n_seq, seq, tc):
    cw = conv_w.shape[2]
    tc = _tile(cw, tc)
    nc = cw // tc
    cb, rb = col0 // tc, row0 // seq

    def part(p):
        return pl.BlockSpec((seq, tc), lambda s, c: (rb + s, cb + p * nc + c))

    return pl.pallas_call(
        _short_conv_kernel,
        grid=(n_seq, nc),
        in_specs=[part(0), part(1), part(2), pl.BlockSpec((1, 3, tc), lambda s, c: (layer, 0, c))],
        out_specs=pl.BlockSpec((seq, tc), lambda s, c: (s, c)),
        out_shape=jax.ShapeDtypeStruct((n_seq * seq, cw), BF16),
        compiler_params=_cparams(("arbitrary", "arbitrary"), 2 * seq * tc * 14 + 6 * seq * tc * 4 + 8 * MIB),
        name="short_conv",
    )(proj, proj, proj, conv_w)


def _hyena_kernel(x0_ref, x1_ref, v_ref, w0_ref, w1_ref, wv_ref, skip_ref, kre_ref, kim_ref,
                  c_ref, s_ref, ct_ref, st_ref, o_ref, g32_ref, g16_ref, acc_ref, *, n_k, scale):
    kk = pl.program_id(2)

    @pl.when(kk == 0)
    def _():
        g = _conv3(x1_ref[...], w1_ref[0]) * _conv3(v_ref[...], wv_ref[0])
        g32_ref[...] = g
        g16_ref[...] = g.astype(BF16)
        acc_ref[...] = jnp.zeros_like(acc_ref)

    g16 = g16_ref[...]
    u_c = jnp.dot(c_ref[...], g16, preferred_element_type=F32)
    u_s = jnp.dot(s_ref[...], g16, preferred_element_type=F32)
    kre, kim = kre_ref[0], kim_ref[0]
    y_re = u_c * kre + u_s * kim
    y_im = u_c * kim - u_s * kre
    acc_ref[...] += (jnp.dot(ct_ref[...], y_re.astype(BF16), preferred_element_type=F32)
                     - jnp.dot(st_ref[...], y_im.astype(BF16), preferred_element_type=F32))

    @pl.when(kk == n_k - 1)
    def _():
        y = acc_ref[...] * scale + skip_ref[0] * g32_ref[...]
        o_ref[...] = (_conv3(x0_ref[...], w0_ref[0]) * y).astype(o_ref.dtype)


def _hyena(proj, hyena_conv_w, hyena_skip, k_re, k_im, tabs, layer, col0, row0, n_seq, seq, tc, kc):
    hw = hyena_skip.shape[2]
    tc, kc = _tile(hw, tc), _tile(seq, kc)
    nc, n_k = hw // tc, seq // kc
    cb, rb = col0 // tc, row0 // seq
    cos_b, sin_b, cos_t, sin_t = tabs

    def part(p):
        return pl.BlockSpec((seq, tc), lambda s, c, k: (rb + s, cb + p * nc + c))

    def wpart(p):
        return pl.BlockSpec((1, 3, tc), lambda s, c, k: (layer, 0, p * nc + c))

    spec_k = pl.BlockSpec((1, kc, tc), lambda s, c, k: (layer, k, c))
    fwd = pl.BlockSpec((kc, seq), lambda s, c, k: (k, 0))
    inv = pl.BlockSpec((seq, kc), lambda s, c, k: (0, k))
    return pl.pallas_call(
        functools.partial(_hyena_kernel, n_k=n_k, scale=1.0 / seq),
        grid=(n_seq, nc, n_k),
        in_specs=[part(0), part(1), part(2), wpart(0), wpart(1), wpart(2),
                  pl.BlockSpec((1, 1, tc), lambda s, c, k: (layer, 0, c)),
                  spec_k, spec_k, fwd, fwd, inv, inv],
        out_specs=pl.BlockSpec((seq, tc), lambda s, c, k: (s, c)),
        out_shape=jax.ShapeDtypeStruct((n_seq * seq, hw), BF16),
        scratch_shapes=[pltpu.VMEM((seq, tc), F32), pltpu.VMEM((seq, tc), BF16), pltpu.VMEM((seq, tc), F32)],
        compiler_params=_cparams(("arbitrary", "arbitrary", "arbitrary"),
                                 2 * (3 * seq * tc * 4 + seq * tc * 2 + 2 * kc * tc * 4 + 4 * kc * seq * 2)
                                 + seq * tc * 10 + 6 * seq * tc * 4 + 8 * MIB),
        name="hyena",
    )(proj, proj, proj, hyena_conv_w, hyena_conv_w, hyena_conv_w, hyena_skip, k_re, k_im, cos_b, sin_b, cos_t, sin_t)


def _filter_kernel(z_ref, w1_ref, b1_ref, fr_ref, w2_ref, b2_ref, w3f_ref, w3b_ref, decay_ref, c_ref, s_ref,
                   kre_ref, kim_ref, hp_ref, hm_ref):
    @pl.when(pl.program_id(2) == 0)
    def _():
        fr = fr_ref[0]
        hdn = jnp.sin(fr * (jnp.dot(z_ref[...], w1_ref[0], precision=HI, preferred_element_type=F32) + b1_ref[0]))
        hdn = jnp.sin(fr * (jnp.dot(hdn, w2_ref[0], precision=HI, preferred_element_type=F32) + b2_ref[0]))
        decay = decay_ref[...]
        h_fwd = jnp.dot(hdn, w3f_ref[0], precision=HI, preferred_element_type=F32) * decay
        h_bwd = jnp.dot(hdn, w3b_ref[0], precision=HI, preferred_element_type=F32) * decay
        row = lax.broadcasted_iota(jnp.int32, h_bwd.shape, 0)
        h_bwd = jnp.where(row == 0, 0.0, h_bwd)
        hp_ref[...] = h_fwd + h_bwd
        hm_ref[...] = h_fwd - h_bwd

    kre_ref[0] = jnp.dot(c_ref[...], hp_ref[...], precision=HI, preferred_element_type=F32)
    kim_ref[0] = -jnp.dot(s_ref[...], hm_ref[...], precision=HI, preferred_element_type=F32)


def _filter_spectra(z, w1, b1, fr, w2, b2, w3, decay, cos32, sin32, tc, kc):
    depth = w1.shape[0]
    seq, hw = decay.shape
    tc, kc = _tile(hw, tc), _tile(seq, kc)
    nc, n_k = hw // tc, seq // kc
    small = lambda shape: pl.BlockSpec((1,) + shape, lambda l, c, k: (l, 0, 0))
    out = pl.BlockSpec((1, kc, tc), lambda l, c, k: (l, k, c))
    tab = pl.BlockSpec((kc, seq), lambda l, c, k: (k, 0))
    return pl.pallas_call(
        _filter_kernel,
        grid=(depth, nc, n_k),
        in_specs=[pl.BlockSpec((seq, FILT_PAD), lambda l, c, k: (0, 0)),
                  small((FILT_PAD, FILT_PAD)), small((1, FILT_PAD)), small((1, FILT_PAD)),
                  small((FILT_PAD, FILT_PAD)), small((1, FILT_PAD)),
                  pl.BlockSpec((1, FILT_PAD, tc), lambda l, c, k: (l, 0, c)),
                  pl.BlockSpec((1, FILT_PAD, tc), lambda l, c, k: (l, 0, nc + c)),
                  pl.BlockSpec((seq, tc), lambda l, c, k: (0, c)),
                  tab, tab],
        out_specs=[out, out],
        out_shape=[jax.ShapeDtypeStruct((depth, seq, hw), F32)] * 2,
        scratch_shapes=[pltpu.VMEM((seq, tc), F32), pltpu.VMEM((seq, tc), F32)],
        compiler_params=_cparams(("arbitrary", "arbitrary", "arbitrary"),
                                 2 * (seq * tc * 4 + 2 * kc * seq * 4 + 2 * kc * tc * 4 + seq * FILT_PAD * 4)
                                 + 6 * seq * tc * 4 + 8 * MIB),
        name="filter_spectra",
    )(z, w1, b1, fr, w2, b2, w3, w3, decay, cos32, sin32)


def _rope_tables(seq):
    rows = seq // GRID_W
    t_row = jnp.repeat(jnp.arange(rows, dtype=F32), GRID_W)
    t_col = jnp.tile(jnp.arange(GRID_W, dtype=F32), rows)
    inv = ROPE_BASE ** (-jnp.arange(0, AXIS_DIM, 2, dtype=F32) / AXIS_DIM)
    c_r, s_r = jnp.cos(t_row[:, None] * inv), jnp.sin(t_row[:, None] * inv)
    c_c, s_c = jnp.cos(t_col[:, None] * inv), jnp.sin(t_col[:, None] * inv)
    cos = jnp.concatenate([c_r, c_r, c_c, c_c], axis=-1)
    sin_signed = jnp.concatenate([-s_r, s_r, -s_c, s_c], axis=-1)
    return cos, sin_signed


def _filter_features(seq):
    pos = jnp.arange(seq, dtype=F32)
    t = jnp.linspace(0.0, 1.0, seq, dtype=F32)
    bands = (FILTER_EMB - 1) // 2
    f = jnp.linspace(1e-4, bands - 1, bands, dtype=F32)
    ang = 2.0 * math.pi * pos[:, None] * f[None, :] / seq
    z = jnp.concatenate([t[:, None], jnp.cos(ang), -jnp.sin(ang)], axis=-1)
    return jnp.pad(z, ((0, 0), (0, FILT_PAD - FILTER_EMB)))


def _filter_decay(seq, width):
    t = jnp.linspace(0.0, 1.0, seq, dtype=F32)
    deltas = jnp.abs(jnp.linspace(math.log(HYENA_TARGET) / SLOW_DECAY_PCT,
                                  math.log(HYENA_TARGET) / FAST_DECAY_PCT, width, dtype=F32))
    return jnp.exp(-t[:, None] * deltas[None, :])


def _odd_dft_tables(seq):
    k = jnp.arange(seq, dtype=jnp.int32)[:, None]
    t = jnp.arange(seq, dtype=jnp.int32)[None, :]
    phase = ((2 * k + 1) * t) % (4 * seq)
    ang = phase.astype(F32) * (math.pi / (2 * seq))
    return jnp.cos(ang), jnp.sin(ang)


def kernel(x_prompt, x_sample, cache_k, cache_v, c, c_ctx, norm_g, w_mod, b_mod, w_in, attn_sink, conv_w, hyena_conv_w,
           filt_w1, filt_b1, filt_freq, filt_w2, filt_b2, filt_w3, hyena_skip, w_branch, w_gate, b_gate, w_o, w_ffn_in,
           w_ffn_out):
    n_ctx, s_ctx, d = x_prompt.shape
    n_lat, s_lat, _ = x_sample.shape
    depth = w_in.shape[0]
    cw = conv_w.shape[2]
    hw = hyena_skip.shape[1]
    d_ff = w_ffn_out.shape[1]
    past = cache_k.shape[2]
    m_ctx, m_lat = n_ctx * s_ctx, n_lat * s_lat
    m = m_ctx + m_lat
    tm_row = s_ctx
    assert s_lat % tm_row == 0 and n_lat + 1 <= MOD_ROWS
    col_conv = ATTN_WIDTH + 2 * KV_WIDTH
    col_hy = col_conv + 3 * cw

    def seq_of_tile(i):
        n_ctx_tiles, per_lat = m_ctx // tm_row, s_lat // tm_row
        return jnp.where(i < n_ctx_tiles, 0, 1 + (i - n_ctx_tiles) // per_lat)

    ffp = _round_up(d_ff, 1024)
    w_in_b = w_in.astype(BF16)
    w_gate_b = w_gate.astype(BF16)
    w_branch_b = w_branch.astype(BF16)
    w_o_b = w_o.astype(BF16)
    pad_cols = ((0, 0), (0, 0), (0, ffp - d_ff))
    w_ffn_in_b = jnp.concatenate([jnp.pad(w_ffn_in[:, :, :d_ff].astype(BF16), pad_cols),
                                  jnp.pad(w_ffn_in[:, :, d_ff:].astype(BF16), pad_cols)], axis=-1)
    w_ffn_out_b = jnp.pad(w_ffn_out.astype(BF16), ((0, 0), (0, ffp - d_ff), (0, 0)))

    cvecs = jnp.concatenate([c_ctx[None], c, jnp.zeros((MOD_ROWS - 1 - n_lat, d), F32)], axis=0)
    mod_tab = _modulations(cvecs, w_mod, b_mod).reshape(depth * MOD_ROWS * N_MOD, 1, d)
    norm_tab = norm_g.reshape(depth * 4, 1, d)

    fp = FILT_PAD - filt_w1.shape[2]
    f_w1 = jnp.pad(filt_w1, ((0, 0), (0, FILT_PAD - FILTER_EMB), (0, fp)))
    f_b1 = jnp.pad(filt_b1, ((0, 0), (0, fp)))[:, None]
    f_fr = jnp.pad(filt_freq, ((0, 0), (0, fp)))[:, None]
    f_w2 = jnp.pad(filt_w2, ((0, 0), (0, fp), (0, fp)))
    f_b2 = jnp.pad(filt_b2, ((0, 0), (0, fp)))[:, None]
    f_w3 = jnp.pad(filt_w3, ((0, 0), (0, fp), (0, 0)))
    spectra, dft = {}, {}
    for seq in sorted({s_ctx, s_lat}):
        cos32, sin32 = _odd_dft_tables(seq)
        spectra[seq] = _filter_spectra(_filter_features(seq), f_w1, f_b1, f_fr, f_w2, f_b2, f_w3,
                                       _filter_decay(seq, hw), cos32, sin32, 512, 256)
        cos_b, sin_b = cos32.astype(BF16), sin32.astype(BF16)
        dft[seq] = (cos_b, sin_b, cos_b.T, sin_b.T)

    rope_cos, rope_sin = _rope_tables(s_lat)
    cache_k2 = cache_k.reshape(n_lat, depth, past, KV_WIDTH)
    cache_v2 = cache_v.reshape(n_lat, depth, past, KV_WIDTH)
    b_gate3 = b_gate.reshape(depth, 1, -1)
    skip3 = hyena_skip.reshape(depth, 1, hw)

    x = jnp.concatenate([x_prompt.reshape(m_ctx, d), x_sample.reshape(m_lat, d)], axis=0)
    (h,) = _resnorm(x, None, norm_tab, mod_tab, seq_of_tile, tm_row, h_norm=0, h_scale=(0, 1), h_shift=(0, 0))
    new_k, new_v = [], []
    for l in range(depth):
        proj = _matmul(h, w_in_b, l, F32, 1024, 1024)
        gates = _gates(h, w_gate_b, b_gate3, l, 1024, 1024)
        new_k.append(proj[:m_ctx, ATTN_WIDTH:ATTN_WIDTH + KV_WIDTH].reshape(n_ctx, s_ctx, N_KV_HEADS, HEAD_DIM))
        new_v.append(proj[:m_ctx, ATTN_WIDTH + KV_WIDTH:col_conv].reshape(n_ctx, s_ctx, N_KV_HEADS, HEAD_DIM))

        o_attn = jnp.concatenate([
            _ctx_attention(proj, attn_sink, l, n_ctx, s_ctx),
            _lat_attention(proj, cache_k2, cache_v2, attn_sink, rope_cos, rope_sin, l, m_ctx, n_lat, s_lat)], axis=0)
        o_conv = jnp.concatenate([
            _short_conv(proj, conv_w, l, col_conv, 0, n_ctx, s_ctx, 512),
            _short_conv(proj, conv_w, l, col_conv, m_ctx, n_lat, s_lat, 256)], axis=0)
        o_hy = jnp.concatenate([
            _hyena(proj, hyena_conv_w, skip3, *spectra[s_ctx], dft[s_ctx], l, col_hy, 0, n_ctx, s_ctx, 512, 512),
            _hyena(proj, hyena_conv_w, skip3, *spectra[s_lat], dft[s_lat], l, col_hy, m_ctx, n_lat, s_lat, 256, 512)],
            axis=0)

        merged = _merge(o_attn, o_conv, o_hy, w_branch_b, gates, l, 1024, 512)
        mix = _matmul(merged, w_o_b, l, F32, 1024, 1024)
        x, h = _resnorm(x, mix, norm_tab, mod_tab, seq_of_tile, tm_row, y_norm=l * 4 + 1, gate=(l, 2),
                        h_norm=l * 4 + 2, h_scale=(l, 4), h_shift=(l, 3))
        hidden = _ffn_in(h, w_ffn_in_b, l, 1024, 512)
        f = _matmul_ksplit(hidden, w_ffn_out_b, l, 1024, 512, 2)
        if l + 1 < depth:
            x, h = _resnorm(x, f, norm_tab, mod_tab, seq_of_tile, tm_row, y_norm=l * 4 + 3, gate=(l, 5),
                            h_norm=(l + 1) * 4, h_scale=(l + 1, 1), h_shift=(l + 1, 0))
        else:
            (x,) = _resnorm(x, f, norm_tab, mod_tab, seq_of_tile, tm_row, y_norm=l * 4 + 3, gate=(l, 5))

    y_prompt = x[:m_ctx].reshape(n_ctx, s_ctx, d)
    y_sample = x[m_ctx:].reshape(n_lat, s_lat, d)
    return y_prompt, y_sample, jnp.stack(new_k, axis=1), jnp.stack(new_v, axis=1)
```

```python
import functools
import math

import jax
import jax.numpy as jnp
from jax import lax
from jax.experimental import pallas as pl
from jax.experimental.pallas import tpu as pltpu

F32 = jnp.float32
BF16 = jnp.bfloat16

HEAD_DIM = 128
N_HEADS = 16
N_KV_HEADS = 4
GQA_GROUP = N_HEADS // N_KV_HEADS
ATTN_WIDTH = N_HEADS * HEAD_DIM
KV_WIDTH = N_KV_HEADS * HEAD_DIM
WINDOW = 128
BLOCK = 128
GRID_W = 64
ROPE_BASE = 10000.0
AXIS_DIM = HEAD_DIM // 2
AXIS_PAIRS = AXIS_DIM // 2
FILTER_EMB = 33
HYENA_TARGET = 1e-2
FAST_DECAY_PCT = 0.3
SLOW_DECAY_PCT = 1.5
EPS = 1e-6
SCALE = HEAD_DIM ** -0.5
LOG2E = math.log2(math.e)
NEG_INF = -1e30
N_MOD = 6
MOD_ROWS = 8
FILT_PAD = 128

LANE = 128
MXU_COLS = 256
MIB = 1024 * 1024
VMEM_CAP_BYTES = 56 * MIB
HI = lax.Precision.HIGHEST


def _cparams(semantics, vmem_bytes):
    return pltpu.CompilerParams(dimension_semantics=semantics,
                                vmem_limit_bytes=int(min(max(vmem_bytes, 16 * MIB), VMEM_CAP_BYTES)))


def _tile(n, pref):
    if n <= pref:
        return n
    t = (pref // LANE) * LANE
    while t > LANE and n % t:
        t -= LANE
    assert n % t == 0, (n, pref)
    return t


def _skip_alias(body, n_in, n_alias, *refs):
    body(*refs[:n_in], *refs[n_in + n_alias:])


def _reuse(body, args, specs, into):
    if not into:
        return body, {}
    n_in = len(args)
    aliases = {n_in + k: k for k in range(len(into))}
    args += list(into)
    specs += [pl.BlockSpec(memory_space=pl.ANY)] * len(into)
    return functools.partial(_skip_alias, body, n_in, len(into)), aliases


def _mods_kernel(c_ref, w_ref, b_ref, o_ref):
    c = c_ref[...]
    a = (c * jax.nn.sigmoid(c)).astype(BF16)
    o_ref[0] = jnp.dot(a, w_ref[0].astype(BF16), preferred_element_type=F32) + b_ref[0]


def _modulations(cvecs, w_mod, b_mod):
    depth, d, n = w_mod.shape
    tn = _tile(n, 512)
    return pl.pallas_call(
        _mods_kernel,
        grid=(depth, n // tn),
        in_specs=[
            pl.BlockSpec((MOD_ROWS, d), lambda l, j: (0, 0)),
            pl.BlockSpec((1, d, tn), lambda l, j: (l, 0, j)),
            pl.BlockSpec((1, 1, tn), lambda l, j: (l, 0, j)),
        ],
        out_specs=pl.BlockSpec((1, MOD_ROWS, tn), lambda l, j: (l, 0, j)),
        out_shape=jax.ShapeDtypeStruct((depth, MOD_ROWS, n), F32),
        compiler_params=_cparams(("arbitrary", "arbitrary"), 2 * d * tn * 4 + 8 * MIB),
        name="modulations",
    )(cvecs, w_mod, b_mod.reshape(depth, 1, n))


def _rms(x):
    return x * lax.rsqrt(jnp.mean(x * x, axis=-1, keepdims=True) + EPS)


def _resnorm_kernel(*refs, has_y, has_h):
    it = iter(refs)
    x_ref = next(it)
    if has_y:
        y_ref, gny_ref, gate_ref = next(it), next(it), next(it)
    if has_h:
        gnx_ref, sc_ref, sh_ref = next(it), next(it), next(it)
    if has_y:
        xo_ref = next(it)
    if has_h:
        h_ref = next(it)
    x = x_ref[...]
    if has_y:
        x = x + gate_ref[0] * (_rms(y_ref[...].astype(F32)) * gny_ref[0])
        xo_ref[...] = x
    if has_h:
        h_ref[...] = ((_rms(x) * gnx_ref[0]) * (1.0 + sc_ref[0]) + sh_ref[0]).astype(h_ref.dtype)


def _resnorm(x, y, norm_tab, mod_tab, seq_of_tile, tm, *, rows, x_base=0, out_base=0, out_rows=None, into=None,
             y_norm=None, gate=None, h_norm=None, h_scale=None, h_shift=None):
    g0, n_rows = rows
    d = x.shape[1]
    out_rows = n_rows if out_rows is None else out_rows
    has_y, has_h = y is not None, h_norm is not None
    assert g0 % tm == 0 and n_rows % tm == 0 and x_base % tm == 0 and out_base % tm == 0
    xt, gt, ot = (g0 - x_base) // tm, g0 // tm, (g0 - out_base) // tm
    x_row = pl.BlockSpec((tm, d), lambda i: (i + xt, 0))
    y_row = pl.BlockSpec((tm, d), lambda i: (i + gt, 0))
    o_row = pl.BlockSpec((tm, d), lambda i: (i + ot, 0))

    def tab(idx):
        return pl.BlockSpec((1, 1, d), lambda i: (idx, 0, 0))

    def mod(layer_which):
        layer, which = layer_which
        return pl.BlockSpec((1, 1, d), lambda i: ((layer * MOD_ROWS + seq_of_tile(i + gt)) * N_MOD + which, 0, 0))

    args, specs = [x], [x_row]
    if has_y:
        args += [y, norm_tab, mod_tab]
        specs += [y_row, tab(y_norm), mod(gate)]
    if has_h:
        args += [norm_tab, mod_tab, mod_tab]
        specs += [tab(h_norm), mod(h_scale), mod(h_shift)]
    out_shape, out_specs = [], []
    if has_y:
        out_shape.append(jax.ShapeDtypeStruct((out_rows, d), F32))
        out_specs.append(o_row)
    if has_h:
        out_shape.append(jax.ShapeDtypeStruct((out_rows, d), BF16))
        out_specs.append(o_row)
    body, aliases = _reuse(functools.partial(_resnorm_kernel, has_y=has_y, has_h=has_h), args, specs, into)
    return pl.pallas_call(
        body,
        grid=(n_rows // tm,),
        in_specs=specs,
        out_specs=out_specs,
        out_shape=out_shape,
        input_output_aliases=aliases,
        compiler_params=_cparams(("arbitrary",), 2 * tm * d * 14 + 8 * MIB),
        name="resnorm",
    )(*args)


def _mm_kernel(a_ref, w_ref, o_ref):
    o_ref[...] = jnp.dot(a_ref[...], w_ref[0], preferred_element_type=F32).astype(o_ref.dtype)


def _matmul(a, w, layer, out_dtype, tm, tn):
    m, k = a.shape
    n = w.shape[2]
    tm, tn = _tile(m, tm), _tile(n, tn)
    ob = jnp.dtype(out_dtype).itemsize
    return pl.pallas_call(
        _mm_kernel,
        grid=(m // tm, n // tn),
        in_specs=[pl.BlockSpec((tm, k), lambda i, j: (i, 0)),
                  pl.BlockSpec((1, k, tn), lambda i, j: (layer, 0, j))],
        out_specs=pl.BlockSpec((tm, tn), lambda i, j: (i, j)),
        out_shape=jax.ShapeDtypeStruct((m, n), out_dtype),
        compiler_params=_cparams(("arbitrary", "arbitrary"),
                                 2 * (tm * k * 2 + k * tn * 2 + tm * tn * ob) + 2 * tm * tn * 4 + 8 * MIB),
        name="matmul",
    )(a, w)


def _gate_kernel(a_ref, w_ref, b_ref, o_ref):
    acc = jnp.dot(a_ref[...], w_ref[0], preferred_element_type=F32) + b_ref[0]
    o_ref[...] = jax.nn.sigmoid(acc).astype(o_ref.dtype)


def _gates(h, w_gate, b_gate, layer, tm, tn):
    m, k = h.shape
    n = w_gate.shape[2]
    tm, tn = _tile(m, tm), _tile(n, tn)
    return pl.pallas_call(
        _gate_kernel,
        grid=(m // tm, n // tn),
        in_specs=[pl.BlockSpec((tm, k), lambda i, j: (i, 0)),
                  pl.BlockSpec((1, k, tn), lambda i, j: (layer, 0, j)),
                  pl.BlockSpec((1, 1, tn), lambda i, j: (layer, 0, j))],
        out_specs=pl.BlockSpec((tm, tn), lambda i, j: (i, j)),
        out_shape=jax.ShapeDtypeStruct((m, n), BF16),
        compiler_params=_cparams(("arbitrary", "arbitrary"),
                                 2 * (tm * k * 2 + k * tn * 2 + tm * tn * 2) + 2 * tm * tn * 4 + 8 * MIB),
        name="gates",
    )(h, w_gate, b_gate)


def _merge_kernel(oa_ref, oc_ref, oh_ref, wa_ref, wc_ref, wh_ref, ga_ref, gc_ref, gh_ref, o_ref):
    ba = jnp.dot(oa_ref[...], wa_ref[0], preferred_element_type=F32)
    bc = jnp.dot(oc_ref[...], wc_ref[0], preferred_element_type=F32)
    bh = jnp.dot(oh_ref[...], wh_ref[0], preferred_element_type=F32)
    merged = ga_ref[...].astype(F32) * ba + gc_ref[...].astype(F32) * bc + gh_ref[...].astype(F32) * bh
    o_ref[...] = merged.astype(o_ref.dtype)


def _merge(o_attn, o_conv, o_hy, w_branch, gates, layer, tm, tn):
    m = o_attn.shape[0]
    d = w_branch.shape[2]
    cw = o_conv.shape[1]
    assert ATTN_WIDTH % cw == 0
    tm, tn = _tile(m, tm), _tile(d, tn)
    nj = d // tn
    conv_blk = ATTN_WIDTH // cw
    return pl.pallas_call(
        _merge_kernel,
        grid=(m // tm, nj),
        in_specs=[pl.BlockSpec((tm, ATTN_WIDTH), lambda i, j: (i, 0)),
                  pl.BlockSpec((tm, cw), lambda i, j: (i, 0)),
                  pl.BlockSpec((tm, cw), lambda i, j: (i, 0)),
                  pl.BlockSpec((1, ATTN_WIDTH, tn), lambda i, j: (layer, 0, j)),
                  pl.BlockSpec((1, cw, tn), lambda i, j: (layer, conv_blk, j)),
                  pl.BlockSpec((1, cw, tn), lambda i, j: (layer, conv_blk + 1, j)),
                  pl.BlockSpec((tm, tn), lambda i, j: (i, j)),
                  pl.BlockSpec((tm, tn), lambda i, j: (i, nj + j)),
                  pl.BlockSpec((tm, tn), lambda i, j: (i, 2 * nj + j))],
        out_specs=pl.BlockSpec((tm, tn), lambda i, j: (i, j)),
        out_shape=jax.ShapeDtypeStruct((m, d), BF16),
        compiler_params=_cparams(("arbitrary", "arbitrary"),
                                 2 * (tm * (ATTN_WIDTH + 2 * cw) * 2 + (ATTN_WIDTH + 2 * cw) * tn * 2 + 4 * tm * tn * 2)
                                 + 3 * tm * tn * 4 + 8 * MIB),
        name="merge",
    )(o_attn, o_conv, o_hy, w_branch, w_branch, w_branch, gates, gates, gates)


def _ffn_in_kernel(h_ref, *refs, n_sub):
    w_refs, o_ref = refs[:-1], refs[-1]
    h = h_ref[...]
    for s in range(n_sub):
        a = jnp.dot(h, w_refs[s][0], preferred_element_type=F32)
        b = jnp.dot(h, w_refs[n_sub + s][0], preferred_element_type=F32)
        o_ref[:, s * MXU_COLS:(s + 1) * MXU_COLS] = (a * jax.nn.sigmoid(a) * b).astype(o_ref.dtype)


def _ffn_in(h, w_ffn_in, layer, tm, n_sub):
    m, k = h.shape
    ff = w_ffn_in.shape[2] // 2
    assert ff % MXU_COLS == 0
    tm = _tile(m, tm)
    nb = ff // MXU_COLS
    tn = n_sub * MXU_COLS
    w_spec = lambda first, s: pl.BlockSpec(
        (1, k, MXU_COLS), lambda i, j: (layer, 0, jnp.minimum(first + n_sub * j + s, 2 * nb - 1)))
    return pl.pallas_call(
        functools.partial(_ffn_in_kernel, n_sub=n_sub),
        grid=(m // tm, pl.cdiv(ff, tn)),
        in_specs=[pl.BlockSpec((tm, k), lambda i, j: (i, 0))]
        + [w_spec(0, s) for s in range(n_sub)] + [w_spec(nb, s) for s in range(n_sub)],
        out_specs=pl.BlockSpec((tm, tn), lambda i, j: (i, j)),
        out_shape=jax.ShapeDtypeStruct((m, ff), BF16),
        compiler_params=_cparams(("arbitrary", "arbitrary"),
                                 2 * (tm * k * 2 + 2 * k * tn * 2 + tm * tn * 2) + 2 * tm * tn * 4 + 8 * MIB),
        name="ffn_in",
    )(h, *([w_ffn_in] * (2 * n_sub)))


def _mm_kacc_kernel(a_ref, w_ref, o_ref, acc_ref, *, n_k):
    part = jnp.dot(a_ref[...], w_ref[0], preferred_element_type=F32)
    kk = pl.program_id(2)

    @pl.when(kk == 0)
    def _():
        acc_ref[...] = part

    @pl.when((kk > 0) & (kk < n_k - 1))
    def _():
        acc_ref[...] += part

    @pl.when(kk == n_k - 1)
    def _():
        o_ref[...] = (acc_ref[...] + part).astype(o_ref.dtype)


def _matmul_ksplit(a, w, layer, out_dtype, tm, tn, n_k):
    m, k = a.shape
    n = w.shape[2]
    tm, tn = _tile(m, tm), _tile(n, tn)
    assert n_k >= 2 and k % n_k == 0 and (k // n_k) % LANE == 0
    tk = k // n_k
    ob = jnp.dtype(out_dtype).itemsize
    return pl.pallas_call(
        functools.partial(_mm_kacc_kernel, n_k=n_k),
        grid=(m // tm, n // tn, n_k),
        in_specs=[pl.BlockSpec((tm, tk), lambda i, j, kk: (i, kk)),
                  pl.BlockSpec((1, tk, tn), lambda i, j, kk: (layer, kk, j))],
        out_specs=pl.BlockSpec((tm, tn), lambda i, j, kk: (i, j)),
        out_shape=jax.ShapeDtypeStruct((m, n), out_dtype),
        scratch_shapes=[pltpu.VMEM((tm, tn), F32)],
        compiler_params=_cparams(("arbitrary", "arbitrary", "arbitrary"),
                                 2 * (tm * tk * 2 + tk * tn * 2 + tm * tn * ob) + 3 * tm * tn * 4 + 8 * MIB),
        name="matmul_ksplit",
    )(a, w)


def _group_attention(q_heads, keys, vals, sinks, bias):
    rows = q_heads[0].shape[0]
    q = jnp.concatenate([qh.astype(BF16) for qh in q_heads], axis=0)
    s = lax.dot_general(q, keys, (((1,), (1,)), ((), ())), preferred_element_type=F32)
    probs, denoms = [], []
    for g, sk in enumerate(sinks):
        sg = s[g * rows:(g + 1) * rows]
        if bias is not None:
            sg = sg + bias
        sink = sk * LOG2E
        m = jnp.maximum(jnp.max(sg, axis=-1, keepdims=True), sink)
        p = jnp.exp2(sg - m)
        denoms.append(jnp.sum(p, axis=-1, keepdims=True) + jnp.exp2(sink - m))
        probs.append(p.astype(BF16))
    o = jnp.dot(jnp.concatenate(probs, axis=0), vals, preferred_element_type=F32)
    return [o[g * rows:(g + 1) * rows] / denoms[g] for g in range(len(q_heads))]


def _ctx_attn_kernel(sink_ref, q_ref, k_ref, v_ref, o_ref, *, layer):
    for hk in range(N_KV_HEADS):
        cols = slice(hk * HEAD_DIM, (hk + 1) * HEAD_DIM)
        heads = range(hk * GQA_GROUP, (hk + 1) * GQA_GROUP)
        q_heads = [q_ref[:, h * HEAD_DIM:(h + 1) * HEAD_DIM] * (SCALE * LOG2E) for h in heads]
        outs = _group_attention(q_heads, k_ref[:, cols].astype(BF16), v_ref[:, cols].astype(BF16),
                                [sink_ref[layer, h] for h in heads], None)
        for h, o in zip(heads, outs):
            o_ref[:, h * HEAD_DIM:(h + 1) * HEAD_DIM] = o.astype(o_ref.dtype)


def _ctx_attention(proj, sink, layer, n_seq, seq, out_rows):
    kb = ATTN_WIDTH // KV_WIDTH
    return pl.pallas_call(
        functools.partial(_ctx_attn_kernel, layer=layer),
        grid=(n_seq,),
        in_specs=[pl.BlockSpec(memory_space=pltpu.SMEM),
                  pl.BlockSpec((seq, ATTN_WIDTH), lambda b: (b, 0)),
                  pl.BlockSpec((seq, KV_WIDTH), lambda b: (b, kb)),
                  pl.BlockSpec((seq, KV_WIDTH), lambda b: (b, kb + 1))],
        out_specs=pl.BlockSpec((seq, ATTN_WIDTH), lambda b: (b, 0)),
        out_shape=jax.ShapeDtypeStruct((out_rows, ATTN_WIDTH), BF16),
        compiler_params=_cparams(("arbitrary",), 32 * MIB),
        name="ctx_attention",
    )(sink, proj, proj, proj)


def _rope(x, cos, sin_signed, lo_half):
    partner = jnp.where(lo_half, pltpu.roll(x, HEAD_DIM - AXIS_PAIRS, 1), pltpu.roll(x, AXIS_PAIRS, 1))
    return x * cos + partner * sin_signed


def _lat_attn_kernel(sink_ref, q_ref, kp_ref, kc_ref, kn_ref, vp_ref, vc_ref, vn_ref, ck_ref, cv_ref,
                     cq_ref, sq_ref, cp_ref, sp_ref, cn_ref, sn_ref, o_ref, *, layer, n_blocks):
    n = pl.program_id(1)
    past = ck_ref.shape[2]
    nk = 3 * BLOCK + past
    lane = lax.broadcasted_iota(jnp.int32, (BLOCK, HEAD_DIM), 1)
    lo_half = (lane % AXIS_DIM) < AXIS_PAIRS
    qi = lax.broadcasted_iota(jnp.int32, (BLOCK, nk), 0)
    kj = lax.broadcasted_iota(jnp.int32, (BLOCK, nk), 1)
    first_ok = jnp.where(n > 0, qi, BLOCK)
    last_ok = jnp.where(n < n_blocks - 1, qi, -1)
    bias_prev = jnp.where(kj >= first_ok, 0.0, NEG_INF)
    bias_next = jnp.where(kj - 2 * BLOCK <= last_ok, 0.0, NEG_INF)
    bias = jnp.where(kj < BLOCK, bias_prev, jnp.where(kj < 2 * BLOCK, 0.0, jnp.where(kj < 3 * BLOCK, bias_next, 0.0)))
    cq, sq = cq_ref[...], sq_ref[...]
    cq_scaled, sq_scaled = cq * (SCALE * LOG2E), sq * (SCALE * LOG2E)
    for hk in range(N_KV_HEADS):
        cols = slice(hk * HEAD_DIM, (hk + 1) * HEAD_DIM)
        heads = range(hk * GQA_GROUP, (hk + 1) * GQA_GROUP)
        keys = jnp.concatenate([
            _rope(kp_ref[:, cols], cp_ref[...], sp_ref[...], lo_half).astype(BF16),
            _rope(kc_ref[:, cols], cq, sq, lo_half).astype(BF16),
            _rope(kn_ref[:, cols], cn_ref[...], sn_ref[...], lo_half).astype(BF16),
            ck_ref[0, 0, :, cols].astype(BF16)], axis=0)
        vals = jnp.concatenate([vp_ref[:, cols].astype(BF16), vc_ref[:, cols].astype(BF16),
                                vn_ref[:, cols].astype(BF16), cv_ref[0, 0, :, cols].astype(BF16)], axis=0)
        q_heads = [_rope(q_ref[:, h * HEAD_DIM:(h + 1) * HEAD_DIM], cq_scaled, sq_scaled, lo_half) for h in heads]
        outs = _group_attention(q_heads, keys, vals, [sink_ref[layer, h] for h in heads], bias)
        for h, o in zip(heads, outs):
            o_ref[:, h * HEAD_DIM:(h + 1) * HEAD_DIM] = o.astype(o_ref.dtype)


def _lat_attention(proj, cache_k, cache_v, sink, rope_cos, rope_sin, layer, row0, n_seq, seq, into):
    nb = seq // BLOCK
    kb = ATTN_WIDTH // KV_WIDTH
    b0 = row0 // BLOCK
    past = cache_k.shape[2]

    def blk(shift):
        return lambda b, n: (b0 + b * nb + jnp.clip(n + shift, 0, nb - 1))

    def rows(shift, col):
        f = blk(shift)
        return lambda b, n: (f(b, n), col)

    def tab(shift):
        return pl.BlockSpec((BLOCK, HEAD_DIM), lambda b, n: (jnp.clip(n + shift, 0, nb - 1), 0))

    kv = lambda shift, col: pl.BlockSpec((BLOCK, KV_WIDTH), rows(shift, col))
    cache = pl.BlockSpec((1, 1, past, KV_WIDTH), lambda b, n: (b, layer, 0, 0))
    args = [sink, proj, proj, proj, proj, proj, proj, proj, cache_k, cache_v,
            rope_cos, rope_sin, rope_cos, rope_sin, rope_cos, rope_sin]
    specs = [pl.BlockSpec(memory_space=pltpu.SMEM),
             pl.BlockSpec((BLOCK, ATTN_WIDTH), rows(0, 0)),
             kv(-1, kb), kv(0, kb), kv(1, kb),
             kv(-1, kb + 1), kv(0, kb + 1), kv(1, kb + 1),
             cache, cache,
             tab(0), tab(0), tab(-1), tab(-1), tab(1), tab(1)]
    body, aliases = _reuse(functools.partial(_lat_attn_kernel, layer=layer, n_blocks=nb), args, specs, [into])
    return pl.pallas_call(
        body,
        grid=(n_seq, nb),
        in_specs=specs,
        out_specs=pl.BlockSpec((BLOCK, ATTN_WIDTH), lambda b, n: (b0 + b * nb + n, 0)),
        out_shape=jax.ShapeDtypeStruct(into.shape, into.dtype),
        input_output_aliases=aliases,
        compiler_params=_cparams(("arbitrary", "arbitrary"), 32 * MIB),
        name="latent_attention",
    )(*args)


def _conv3(u, w):
    n_rows = u.shape[0]
    row = lax.broadcasted_iota(jnp.int32, u.shape, 0)
    before = jnp.where(row == 0, 0.0, pltpu.roll(u, 1, 0))
    after = jnp.where(row == n_rows - 1, 0.0, pltpu.roll(u, n_rows - 1, 0))
    return before * w[0:1] + u * w[1:2] + after * w[2:3]


def _short_conv_kernel(b_ref, c_ref, x_ref, w_ref, o_ref):
    o_ref[...] = (b_ref[...] * _conv3(c_ref[...] * x_ref[...], w_ref[0])).astype(o_ref.dtype)


def _short_conv(proj, conv_w, layer, col0, row0, n_seq, seq, tc, out_rows, into=None):
    cw = conv_w.shape[2]
    tc = _tile(cw, tc)
    nc = cw // tc
    cb, rb = col0 // tc, row0 // seq

    def part(p):
        return pl.BlockSpec((seq, tc), lambda s, c: (rb + s, cb + p * nc + c))

    args = [proj, proj, proj, conv_w]
    specs = [part(0), part(1), part(2), pl.BlockSpec((1, 3, tc), lambda s, c: (layer, 0, c))]
    body, aliases = _reuse(_short_conv_kernel, args, specs, [into] if into is not None else None)
    return pl.pallas_call(
        body,
        grid=(n_seq, nc),
        in_specs=specs,
        out_specs=pl.BlockSpec((seq, tc), lambda s, c: (rb + s, c)),
        out_shape=jax.ShapeDtypeStruct((out_rows, cw), BF16),
        input_output_aliases=aliases,
        compiler_params=_cparams(("arbitrary", "arbitrary"), 2 * seq * tc * 14 + 6 * seq * tc * 4 + 8 * MIB),
        name="short_conv",
    )(*args)


def _hyena_kernel(x0_ref, x1_ref, v_ref, w0_ref, w1_ref, wv_ref, skip_ref, kre_ref, kim_ref,
                  c_ref, s_ref, ct_ref, st_ref, o_ref, g32_ref, g16_ref, acc_ref, *, n_k, scale):
    kk = pl.program_id(2)

    @pl.when(kk == 0)
    def _():
        g = _conv3(x1_ref[...], w1_ref[0]) * _conv3(v_ref[...], wv_ref[0])
        g32_ref[...] = g
        g16_ref[...] = g.astype(BF16)
        acc_ref[...] = jnp.zeros_like(acc_ref)

    g16 = g16_ref[...]
    u_c = jnp.dot(c_ref[...], g16, preferred_element_type=F32)
    u_s = jnp.dot(s_ref[...], g16, preferred_element_type=F32)
    kre, kim = kre_ref[0], kim_ref[0]
    y_re = u_c * kre + u_s * kim
    y_im = u_c * kim - u_s * kre
    acc_ref[...] += (jnp.dot(ct_ref[...], y_re.astype(BF16), preferred_element_type=F32)
                     - jnp.dot(st_ref[...], y_im.astype(BF16), preferred_element_type=F32))

    @pl.when(kk == n_k - 1)
    def _():
        y = acc_ref[...] * scale + skip_ref[0] * g32_ref[...]
        o_ref[...] = (_conv3(x0_ref[...], w0_ref[0]) * y).astype(o_ref.dtype)


def _hyena(proj, hyena_conv_w, hyena_skip, k_re, k_im, tabs, layer, col0, row0, n_seq, seq, tc, kc, out_rows,
           into=None):
    hw = hyena_skip.shape[2]
    tc, kc = _tile(hw, tc), _tile(seq, kc)
    nc, n_k = hw // tc, seq // kc
    cb, rb = col0 // tc, row0 // seq
    cos_b, sin_b, cos_t, sin_t = tabs

    def part(p):
        return pl.BlockSpec((seq, tc), lambda s, c, k: (rb + s, cb + p * nc + c))

    def wpart(p):
        return pl.BlockSpec((1, 3, tc), lambda s, c, k: (layer, 0, p * nc + c))

    spec_k = pl.BlockSpec((1, kc, tc), lambda s, c, k: (layer, k, c))
    fwd = pl.BlockSpec((kc, seq), lambda s, c, k: (k, 0))
    inv = pl.BlockSpec((seq, kc), lambda s, c, k: (0, k))
    args = [proj, proj, proj, hyena_conv_w, hyena_conv_w, hyena_conv_w, hyena_skip, k_re, k_im,
            cos_b, sin_b, cos_t, sin_t]
    specs = [part(0), part(1), part(2), wpart(0), wpart(1), wpart(2),
             pl.BlockSpec((1, 1, tc), lambda s, c, k: (layer, 0, c)),
             spec_k, spec_k, fwd, fwd, inv, inv]
    body, aliases = _reuse(functools.partial(_hyena_kernel, n_k=n_k, scale=1.0 / seq), args, specs,
                           [into] if into is not None else None)
    return pl.pallas_call(
        body,
        grid=(n_seq, nc, n_k),
        in_specs=specs,
        out_specs=pl.BlockSpec((seq, tc), lambda s, c, k: (rb + s, c)),
        out_shape=jax.ShapeDtypeStruct((out_rows, hw), BF16),
        input_output_aliases=aliases,
        scratch_shapes=[pltpu.VMEM((seq, tc), F32), pltpu.VMEM((seq, tc), BF16), pltpu.VMEM((seq, tc), F32)],
        compiler_params=_cparams(("arbitrary", "arbitrary", "arbitrary"),
                                 2 * (3 * seq * tc * 4 + seq * tc * 2 + 2 * kc * tc * 4 + 4 * kc * seq * 2)
                                 + seq * tc * 10 + 6 * seq * tc * 4 + 8 * MIB),
        name="hyena",
    )(*args)


def _filter_kernel(z_ref, w1_ref, b1_ref, fr_ref, w2_ref, b2_ref, w3f_ref, w3b_ref, decay_ref, c_ref, s_ref,
                   kre_ref, kim_ref, hp_ref, hm_ref):
    @pl.when(pl.program_id(2) == 0)
    def _():
        fr = fr_ref[0]
        hdn = jnp.sin(fr * (jnp.dot(z_ref[...], w1_ref[0], precision=HI, preferred_element_type=F32) + b1_ref[0]))
        hdn = jnp.sin(fr * (jnp.dot(hdn, w2_ref[0], precision=HI, preferred_element_type=F32) + b2_ref[0]))
        decay = decay_ref[...]
        h_fwd = jnp.dot(hdn, w3f_ref[0], precision=HI, preferred_element_type=F32) * decay
        h_bwd = jnp.dot(hdn, w3b_ref[0], precision=HI, preferred_element_type=F32) * decay
        row = lax.broadcasted_iota(jnp.int32, h_bwd.shape, 0)
        h_bwd = jnp.where(row == 0, 0.0, h_bwd)
        hp_ref[...] = h_fwd + h_bwd
        hm_ref[...] = h_fwd - h_bwd

    kre_ref[0] = jnp.dot(c_ref[...], hp_ref[...], precision=HI, preferred_element_type=F32)
    kim_ref[0] = -jnp.dot(s_ref[...], hm_ref[...], precision=HI, preferred_element_type=F32)


def _filter_spectra(z, w1, b1, fr, w2, b2, w3, decay, cos32, sin32, tc, kc):
    depth = w1.shape[0]
    seq, hw = decay.shape
    tc, kc = _tile(hw, tc), _tile(seq, kc)
    nc, n_k = hw // tc, seq // kc
    small = lambda shape: pl.BlockSpec((1,) + shape, lambda l, c, k: (l, 0, 0))
    out = pl.BlockSpec((1, kc, tc), lambda l, c, k: (l, k, c))
    tab = pl.BlockSpec((kc, seq), lambda l, c, k: (k, 0))
    return pl.pallas_call(
        _filter_kernel,
        grid=(depth, nc, n_k),
        in_specs=[pl.BlockSpec((seq, FILT_PAD), lambda l, c, k: (0, 0)),
                  small((FILT_PAD, FILT_PAD)), small((1, FILT_PAD)), small((1, FILT_PAD)),
                  small((FILT_PAD, FILT_PAD)), small((1, FILT_PAD)),
                  pl.BlockSpec((1, FILT_PAD, tc), lambda l, c, k: (l, 0, c)),
                  pl.BlockSpec((1, FILT_PAD, tc), lambda l, c, k: (l, 0, nc + c)),
                  pl.BlockSpec((seq, tc), lambda l, c, k: (0, c)),
                  tab, tab],
        out_specs=[out, out],
        out_shape=[jax.ShapeDtypeStruct((depth, seq, hw), F32)] * 2,
        scratch_shapes=[pltpu.VMEM((seq, tc), F32), pltpu.VMEM((seq, tc), F32)],
        compiler_params=_cparams(("arbitrary", "arbitrary", "arbitrary"),
                                 2 * (seq * tc * 4 + 2 * kc * seq * 4 + 2 * kc * tc * 4 + seq * FILT_PAD * 4)
                                 + 6 * seq * tc * 4 + 8 * MIB),
        name="filter_spectra",
    )(z, w1, b1, fr, w2, b2, w3, w3, decay, cos32, sin32)


def _rope_tables(seq):
    rows = seq // GRID_W
    t_row = jnp.repeat(jnp.arange(rows, dtype=F32), GRID_W)
    t_col = jnp.tile(jnp.arange(GRID_W, dtype=F32), rows)
    inv = ROPE_BASE ** (-jnp.arange(0, AXIS_DIM, 2, dtype=F32) / AXIS_DIM)
    c_r, s_r = jnp.cos(t_row[:, None] * inv), jnp.sin(t_row[:, None] * inv)
    c_c, s_c = jnp.cos(t_col[:, None] * inv), jnp.sin(t_col[:, None] * inv)
    cos = jnp.concatenate([c_r, c_r, c_c, c_c], axis=-1)
    sin_signed = jnp.concatenate([-s_r, s_r, -s_c, s_c], axis=-1)
    return cos, sin_signed


def _filter_features(seq):
    pos = jnp.arange(seq, dtype=F32)
    t = jnp.linspace(0.0, 1.0, seq, dtype=F32)
    bands = (FILTER_EMB - 1) // 2
    f = jnp.linspace(1e-4, bands - 1, bands, dtype=F32)
    ang = 2.0 * math.pi * pos[:, None] * f[None, :] / seq
    z = jnp.concatenate([t[:, None], jnp.cos(ang), -jnp.sin(ang)], axis=-1)
    return jnp.pad(z, ((0, 0), (0, FILT_PAD - FILTER_EMB)))


def _filter_decay(seq, width):
    t = jnp.linspace(0.0, 1.0, seq, dtype=F32)
    deltas = jnp.abs(jnp.linspace(math.log(HYENA_TARGET) / SLOW_DECAY_PCT,
                                  math.log(HYENA_TARGET) / FAST_DECAY_PCT, width, dtype=F32))
    return jnp.exp(-t[:, None] * deltas[None, :])


def _odd_dft_tables(seq):
    k = jnp.arange(seq, dtype=jnp.int32)[:, None]
    t = jnp.arange(seq, dtype=jnp.int32)[None, :]
    phase = ((2 * k + 1) * t) % (4 * seq)
    ang = phase.astype(F32) * (math.pi / (2 * seq))
    return jnp.cos(ang), jnp.sin(ang)


def kernel(x_prompt, x_sample, cache_k, cache_v, c, c_ctx, norm_g, w_mod, b_mod, w_in, attn_sink, conv_w, hyena_conv_w,
           filt_w1, filt_b1, filt_freq, filt_w2, filt_b2, filt_w3, hyena_skip, w_branch, w_gate, b_gate, w_o, w_ffn_in,
           w_ffn_out):
    n_ctx, s_ctx, d = x_prompt.shape
    n_lat, s_lat, _ = x_sample.shape
    depth = w_in.shape[0]
    cw = conv_w.shape[2]
    hw = hyena_skip.shape[1]
    d_ff = w_ffn_out.shape[1]
    past = cache_k.shape[2]
    m_ctx, m_lat = n_ctx * s_ctx, n_lat * s_lat
    m = m_ctx + m_lat
    tm_row = s_ctx
    assert s_lat % tm_row == 0 and m_ctx % s_lat == 0 and n_lat + 1 <= MOD_ROWS
    col_conv = ATTN_WIDTH + 2 * KV_WIDTH
    col_hy = col_conv + 3 * cw

    def seq_of_tile(i):
        n_ctx_tiles, per_lat = m_ctx // tm_row, s_lat // tm_row
        return jnp.where(i < n_ctx_tiles, 0, 1 + (i - n_ctx_tiles) // per_lat)

    w_in_b = w_in.astype(BF16)
    w_gate_b = w_gate.astype(BF16)
    w_branch_b = w_branch.astype(BF16)
    w_o_b = w_o.astype(BF16)
    w_ffn_in_b = w_ffn_in.astype(BF16)
    w_ffn_out_b = w_ffn_out.astype(BF16)

    cvecs = jnp.concatenate([c_ctx[None], c, jnp.zeros((MOD_ROWS - 1 - n_lat, d), F32)], axis=0)
    mod_tab = _modulations(cvecs, w_mod, b_mod).reshape(depth * MOD_ROWS * N_MOD, 1, d)
    norm_tab = norm_g.reshape(depth * 4, 1, d)

    fp = FILT_PAD - filt_w1.shape[2]
    f_w1 = jnp.pad(filt_w1, ((0, 0), (0, FILT_PAD - FILTER_EMB), (0, fp)))
    f_b1 = jnp.pad(filt_b1, ((0, 0), (0, fp)))[:, None]
    f_fr = jnp.pad(filt_freq, ((0, 0), (0, fp)))[:, None]
    f_w2 = jnp.pad(filt_w2, ((0, 0), (0, fp), (0, fp)))
    f_b2 = jnp.pad(filt_b2, ((0, 0), (0, fp)))[:, None]
    f_w3 = jnp.pad(filt_w3, ((0, 0), (0, fp), (0, 0)))
    spectra, dft = {}, {}
    for seq in sorted({s_ctx, s_lat}):
        cos32, sin32 = _odd_dft_tables(seq)
        spectra[seq] = _filter_spectra(_filter_features(seq), f_w1, f_b1, f_fr, f_w2, f_b2, f_w3,
                                       _filter_decay(seq, hw), cos32, sin32, 512, 256)
        cos_b, sin_b = cos32.astype(BF16), sin32.astype(BF16)
        dft[seq] = (cos_b, sin_b, cos_b.T, sin_b.T)

    rope_cos, rope_sin = _rope_tables(s_lat)
    cache_k2 = cache_k.reshape(n_lat, depth, past, KV_WIDTH)
    cache_v2 = cache_v.reshape(n_lat, depth, past, KV_WIDTH)
    b_gate3 = b_gate.reshape(depth, 1, -1)
    skip3 = hyena_skip.reshape(depth, 1, hw)

    groups = ((0, m_ctx), (m_ctx, m_lat))

    def res_stage(x_parts, y, split_out, **kw):
        if len(x_parts) == 1 and not split_out:
            return _resnorm(x_parts[0][0], y, norm_tab, mod_tab, seq_of_tile, tm_row, rows=(0, m), **kw)
        outs, per_group = None, []
        for gi, (g0, n_rows) in enumerate(groups):
            xa, xb = x_parts[gi] if len(x_parts) > 1 else x_parts[0]
            if split_out:
                per_group.append(_resnorm(xa, y, norm_tab, mod_tab, seq_of_tile, tm_row, rows=(g0, n_rows), x_base=xb,
                                          out_base=g0, **kw))
            else:
                outs = _resnorm(xa, y, norm_tab, mod_tab, seq_of_tile, tm_row, rows=(g0, n_rows), x_base=xb,
                                out_rows=m, into=outs, **kw)
        return per_group if split_out else outs

    x_parts = [(x_prompt.reshape(m_ctx, d), 0), (x_sample.reshape(m_lat, d), m_ctx)]
    (h,) = res_stage(x_parts, None, False, h_norm=0, h_scale=(0, 1), h_shift=(0, 0))
    new_k, new_v = [], []
    for l in range(depth):
        proj = _matmul(h, w_in_b, l, F32, 1024, 1024)
        gates = _gates(h, w_gate_b, b_gate3, l, 1024, 1024)
        new_k.append(proj[:m_ctx, ATTN_WIDTH:ATTN_WIDTH + KV_WIDTH].reshape(n_ctx, s_ctx, N_KV_HEADS, HEAD_DIM))
        new_v.append(proj[:m_ctx, ATTN_WIDTH + KV_WIDTH:col_conv].reshape(n_ctx, s_ctx, N_KV_HEADS, HEAD_DIM))

        o_attn = _ctx_attention(proj, attn_sink, l, n_ctx, s_ctx, m)
        o_attn = _lat_attention(proj, cache_k2, cache_v2, attn_sink, rope_cos, rope_sin, l, m_ctx, n_lat, s_lat, o_attn)
        o_conv = _short_conv(proj, conv_w, l, col_conv, 0, n_ctx, s_ctx, 512, m)
        o_conv = _short_conv(proj, conv_w, l, col_conv, m_ctx, n_lat, s_lat, 256, m, o_conv)
        o_hy = _hyena(proj, hyena_conv_w, skip3, *spectra[s_ctx], dft[s_ctx], l, col_hy, 0, n_ctx, s_ctx, 512, 512, m)
        o_hy = _hyena(proj, hyena_conv_w, skip3, *spectra[s_lat], dft[s_lat], l, col_hy, m_ctx, n_lat, s_lat, 256, 512,
                      m, o_hy)

        merged = _merge(o_attn, o_conv, o_hy, w_branch_b, gates, l, 1024, 512)
        mix = _matmul(merged, w_o_b, l, BF16, 1024, 1024)
        x, h = res_stage(x_parts, mix, False, y_norm=l * 4 + 1, gate=(l, 2),
                         h_norm=l * 4 + 2, h_scale=(l, 4), h_shift=(l, 3))
        x_parts = [(x, 0)]
        hidden = _ffn_in(h, w_ffn_in_b, l, 1024, 2)
        f = _matmul(hidden, w_ffn_out_b, l, BF16, 512, 512)
        if l + 1 < depth:
            x, h = res_stage(x_parts, f, False, y_norm=l * 4 + 3, gate=(l, 5),
                             h_norm=(l + 1) * 4, h_scale=(l + 1, 1), h_shift=(l + 1, 0))
            x_parts = [(x, 0)]
        else:
            (y_prompt,), (y_sample,) = res_stage(x_parts, f, True, y_norm=l * 4 + 3, gate=(l, 5))

    return (y_prompt.reshape(n_ctx, s_ctx, d), y_sample.reshape(n_lat, s_lat, d),
            jnp.stack(new_k, axis=1), jnp.stack(new_v, axis=1))
```

```python
import functools
import math

import jax
import jax.numpy as jnp
from jax import lax
from jax.experimental import pallas as pl
from jax.experimental.pallas import tpu as pltpu

F32 = jnp.float32
BF16 = jnp.bfloat16

HEAD_DIM = 128
N_HEADS = 16
N_KV_HEADS = 4
GQA_GROUP = N_HEADS // N_KV_HEADS
ATTN_WIDTH = N_HEADS * HEAD_DIM
KV_WIDTH = N_KV_HEADS * HEAD_DIM
WINDOW = 128
BLOCK = 128
GRID_W = 64
ROPE_BASE = 10000.0
AXIS_DIM = HEAD_DIM // 2
AXIS_PAIRS = AXIS_DIM // 2
FILTER_EMB = 33
HYENA_TARGET = 1e-2
FAST_DECAY_PCT = 0.3
SLOW_DECAY_PCT = 1.5
EPS = 1e-6
SCALE = HEAD_DIM ** -0.5
LOG2E = math.log2(math.e)
NEG_INF = -1e30
N_MOD = 6
MOD_ROWS = 8
FILT_PAD = 128

LANE = 128
MXU_COLS = 256
BF16_ROWS = 16
MIB = 1024 * 1024
VMEM_CAP_BYTES = 56 * MIB
HI = lax.Precision.HIGHEST


def _cparams(semantics, vmem_bytes):
    return pltpu.CompilerParams(dimension_semantics=semantics,
                                vmem_limit_bytes=int(min(max(vmem_bytes, 16 * MIB), VMEM_CAP_BYTES)))


def _tile(n, pref):
    if n <= pref:
        return n
    t = (pref // LANE) * LANE
    while t > LANE and n % t:
        t -= LANE
    assert n % t == 0, (n, pref)
    return t


def _skip_alias(body, n_in, n_alias, *refs):
    body(*refs[:n_in], *refs[n_in + n_alias:])


def _reuse(body, args, specs, into):
    if not into:
        return body, {}
    n_in = len(args)
    aliases = {n_in + k: k for k in range(len(into))}
    args += list(into)
    specs += [pl.BlockSpec(memory_space=pl.ANY)] * len(into)
    return functools.partial(_skip_alias, body, n_in, len(into)), aliases


def _mods_kernel(c_ref, w_ref, b_ref, o_ref):
    c = c_ref[...]
    a = (c * jax.nn.sigmoid(c)).astype(BF16)
    o_ref[0] = jnp.dot(a, w_ref[0].astype(BF16), preferred_element_type=F32) + b_ref[0]


def _modulations(cvecs, w_mod, b_mod):
    depth, d, n = w_mod.shape
    tn = _tile(n, 512)
    return pl.pallas_call(
        _mods_kernel,
        grid=(depth, n // tn),
        in_specs=[
            pl.BlockSpec((MOD_ROWS, d), lambda l, j: (0, 0)),
            pl.BlockSpec((1, d, tn), lambda l, j: (l, 0, j)),
            pl.BlockSpec((1, 1, tn), lambda l, j: (l, 0, j)),
        ],
        out_specs=pl.BlockSpec((1, MOD_ROWS, tn), lambda l, j: (l, 0, j)),
        out_shape=jax.ShapeDtypeStruct((depth, MOD_ROWS, n), F32),
        compiler_params=_cparams(("arbitrary", "arbitrary"), 2 * d * tn * 4 + 8 * MIB),
        name="modulations",
    )(cvecs, w_mod, b_mod.reshape(depth, 1, n))


def _rms(x):
    return x * lax.rsqrt(jnp.mean(x * x, axis=-1, keepdims=True) + EPS)


def _resnorm_kernel(*refs, has_y, has_h):
    it = iter(refs)
    x_ref = next(it)
    if has_y:
        y_ref, gny_ref, gate_ref = next(it), next(it), next(it)
    if has_h:
        gnx_ref, sc_ref, sh_ref = next(it), next(it), next(it)
    if has_y:
        xo_ref = next(it)
    if has_h:
        h_ref = next(it)
    x = x_ref[...]
    if has_y:
        x = x + gate_ref[0] * (_rms(y_ref[...].astype(F32)) * gny_ref[0])
        xo_ref[...] = x
    if has_h:
        h_ref[...] = ((_rms(x) * gnx_ref[0]) * (1.0 + sc_ref[0]) + sh_ref[0]).astype(h_ref.dtype)


def _resnorm(x, y, norm_tab, mod_tab, seq_of_tile, tm, *, rows, x_base=0, out_base=0, out_rows=None, into=None,
             y_norm=None, gate=None, h_norm=None, h_scale=None, h_shift=None):
    g0, n_rows = rows
    d = x.shape[1]
    out_rows = n_rows if out_rows is None else out_rows
    has_y, has_h = y is not None, h_norm is not None
    assert g0 % tm == 0 and n_rows % tm == 0 and x_base % tm == 0 and out_base % tm == 0
    xt, gt, ot = (g0 - x_base) // tm, g0 // tm, (g0 - out_base) // tm
    x_row = pl.BlockSpec((tm, d), lambda i: (i + xt, 0))
    y_row = pl.BlockSpec((tm, d), lambda i: (i + gt, 0))
    o_row = pl.BlockSpec((tm, d), lambda i: (i + ot, 0))

    def tab(idx):
        return pl.BlockSpec((1, 1, d), lambda i: (idx, 0, 0))

    def mod(layer_which):
        layer, which = layer_which
        return pl.BlockSpec((1, 1, d), lambda i: ((layer * MOD_ROWS + seq_of_tile(i + gt)) * N_MOD + which, 0, 0))

    args, specs = [x], [x_row]
    if has_y:
        args += [y, norm_tab, mod_tab]
        specs += [y_row, tab(y_norm), mod(gate)]
    if has_h:
        args += [norm_tab, mod_tab, mod_tab]
        specs += [tab(h_norm), mod(h_scale), mod(h_shift)]
    out_shape, out_specs = [], []
    if has_y:
        out_shape.append(jax.ShapeDtypeStruct((out_rows, d), F32))
        out_specs.append(o_row)
    if has_h:
        out_shape.append(jax.ShapeDtypeStruct((out_rows, d), BF16))
        out_specs.append(o_row)
    body, aliases = _reuse(functools.partial(_resnorm_kernel, has_y=has_y, has_h=has_h), args, specs, into)
    return pl.pallas_call(
        body,
        grid=(n_rows // tm,),
        in_specs=specs,
        out_specs=out_specs,
        out_shape=out_shape,
        input_output_aliases=aliases,
        compiler_params=_cparams(("arbitrary",), 2 * tm * d * 14 + 8 * MIB),
        name="resnorm",
    )(*args)


def _with_casts(body, n_in, n_cast, *refs):
    srcs = refs[n_in:n_in + n_cast]
    dsts = refs[n_in + n_cast + 1:n_in + 2 * n_cast + 1]
    body(*refs[:n_in], refs[n_in + n_cast], *refs[n_in + 2 * n_cast + 1:])
    for src, dst in zip(srcs, dsts):
        dst[...] = src[...].astype(dst.dtype)


def _dense_call(body, args, specs, grid, out_shape, out_spec, vmem_bytes, name, casts=()):
    nj = grid[1]
    steps = grid[0] * nj
    n_in = len(args)
    args, specs = list(args), list(specs)
    out_shapes, out_specs = [out_shape], [out_spec]
    for w, layer in casts:
        k, cols = w.shape[1], w.shape[2]
        rows = BF16_ROWS
        while k % rows or k // rows > steps:
            rows += BF16_ROWS
        n_blocks = k // rows
        blk = lambda i, j, n_blocks=n_blocks: jnp.minimum(i * nj + j, n_blocks - 1)
        args.append(w.reshape(-1, cols))
        specs.append(pl.BlockSpec((rows, cols), lambda i, j, blk=blk, first=layer * n_blocks: (first + blk(i, j), 0)))
        out_shapes.append(jax.ShapeDtypeStruct((k, cols), BF16))
        out_specs.append(pl.BlockSpec((rows, cols), lambda i, j, blk=blk: (blk(i, j), 0)))
        vmem_bytes += 2 * rows * cols * 6
    kernel = functools.partial(_with_casts, body, n_in, len(casts))
    outs = pl.pallas_call(
        kernel, grid=grid, in_specs=specs, out_specs=out_specs, out_shape=out_shapes,
        compiler_params=_cparams(("arbitrary", "arbitrary"), vmem_bytes), name=name,
    )(*args)
    return outs[0], list(outs[1:])


def _mm_kernel(a_ref, w_ref, o_ref):
    o_ref[...] = jnp.dot(a_ref[...], w_ref[0], preferred_element_type=F32).astype(o_ref.dtype)


def _matmul(a, w, layer, out_dtype, tm, tn, casts=()):
    m, k = a.shape
    n = w.shape[2]
    tm, tn = _tile(m, tm), _tile(n, tn)
    ob = jnp.dtype(out_dtype).itemsize
    return _dense_call(
        _mm_kernel, [a, w],
        [pl.BlockSpec((tm, k), lambda i, j: (i, 0)), pl.BlockSpec((1, k, tn), lambda i, j: (layer, 0, j))],
        (m // tm, n // tn), jax.ShapeDtypeStruct((m, n), out_dtype), pl.BlockSpec((tm, tn), lambda i, j: (i, j)),
        2 * (tm * k * 2 + k * tn * 2 + tm * tn * ob) + 2 * tm * tn * 4 + 8 * MIB, "matmul", casts)


def _gate_kernel(a_ref, w_ref, b_ref, o_ref):
    acc = jnp.dot(a_ref[...], w_ref[0], preferred_element_type=F32) + b_ref[0]
    o_ref[...] = jax.nn.sigmoid(acc).astype(o_ref.dtype)


def _gates(h, w_gate, b_gate, layer, b_layer, tm, tn, casts=()):
    m, k = h.shape
    n = w_gate.shape[2]
    tm, tn = _tile(m, tm), _tile(n, tn)
    return _dense_call(
        _gate_kernel, [h, w_gate, b_gate],
        [pl.BlockSpec((tm, k), lambda i, j: (i, 0)),
         pl.BlockSpec((1, k, tn), lambda i, j: (layer, 0, j)),
         pl.BlockSpec((1, 1, tn), lambda i, j: (b_layer, 0, j))],
        (m // tm, n // tn), jax.ShapeDtypeStruct((m, n), BF16), pl.BlockSpec((tm, tn), lambda i, j: (i, j)),
        2 * (tm * k * 2 + k * tn * 2 + tm * tn * 2) + 2 * tm * tn * 4 + 8 * MIB, "gates", casts)


def _merge_kernel(oa_ref, oc_ref, oh_ref, wa_ref, wc_ref, wh_ref, ga_ref, gc_ref, gh_ref, o_ref):
    ba = jnp.dot(oa_ref[...], wa_ref[0], preferred_element_type=F32)
    bc = jnp.dot(oc_ref[...], wc_ref[0], preferred_element_type=F32)
    bh = jnp.dot(oh_ref[...], wh_ref[0], preferred_element_type=F32)
    merged = ga_ref[...].astype(F32) * ba + gc_ref[...].astype(F32) * bc + gh_ref[...].astype(F32) * bh
    o_ref[...] = merged.astype(o_ref.dtype)


def _merge(o_attn, o_conv, o_hy, w_branch, gates, layer, tm, tn, casts=()):
    m = o_attn.shape[0]
    d = w_branch.shape[2]
    cw = o_conv.shape[1]
    assert ATTN_WIDTH % cw == 0
    tm, tn = _tile(m, tm), _tile(d, tn)
    nj = d // tn
    conv_blk = ATTN_WIDTH // cw
    return _dense_call(
        _merge_kernel, [o_attn, o_conv, o_hy, w_branch, w_branch, w_branch, gates, gates, gates],
        [pl.BlockSpec((tm, ATTN_WIDTH), lambda i, j: (i, 0)),
         pl.BlockSpec((tm, cw), lambda i, j: (i, 0)),
         pl.BlockSpec((tm, cw), lambda i, j: (i, 0)),
         pl.BlockSpec((1, ATTN_WIDTH, tn), lambda i, j: (layer, 0, j)),
         pl.BlockSpec((1, cw, tn), lambda i, j: (layer, conv_blk, j)),
         pl.BlockSpec((1, cw, tn), lambda i, j: (layer, conv_blk + 1, j)),
         pl.BlockSpec((tm, tn), lambda i, j: (i, j)),
         pl.BlockSpec((tm, tn), lambda i, j: (i, nj + j)),
         pl.BlockSpec((tm, tn), lambda i, j: (i, 2 * nj + j))],
        (m // tm, nj), jax.ShapeDtypeStruct((m, d), BF16), pl.BlockSpec((tm, tn), lambda i, j: (i, j)),
        2 * (tm * (ATTN_WIDTH + 2 * cw) * 2 + (ATTN_WIDTH + 2 * cw) * tn * 2 + 4 * tm * tn * 2)
        + 3 * tm * tn * 4 + 8 * MIB, "merge", casts)


def _ffn_in_kernel(h_ref, *refs, n_sub):
    w_refs, o_ref = refs[:-1], refs[-1]
    h = h_ref[...]
    for s in range(n_sub):
        a = jnp.dot(h, w_refs[s][0], preferred_element_type=F32)
        b = jnp.dot(h, w_refs[n_sub + s][0], preferred_element_type=F32)
        o_ref[:, s * MXU_COLS:(s + 1) * MXU_COLS] = (a * jax.nn.sigmoid(a) * b).astype(o_ref.dtype)


def _ffn_in(h, w_ffn_in, layer, tm, n_sub, casts=()):
    m, k = h.shape
    ff = w_ffn_in.shape[2] // 2
    assert ff % MXU_COLS == 0
    tm = _tile(m, tm)
    nb = ff // MXU_COLS
    tn = n_sub * MXU_COLS
    w_spec = lambda first, s: pl.BlockSpec(
        (1, k, MXU_COLS), lambda i, j: (layer, 0, jnp.minimum(first + n_sub * j + s, 2 * nb - 1)))
    return _dense_call(
        functools.partial(_ffn_in_kernel, n_sub=n_sub), [h] + [w_ffn_in] * (2 * n_sub),
        [pl.BlockSpec((tm, k), lambda i, j: (i, 0))]
        + [w_spec(0, s) for s in range(n_sub)] + [w_spec(nb, s) for s in range(n_sub)],
        (m // tm, pl.cdiv(ff, tn)), jax.ShapeDtypeStruct((m, ff), BF16), pl.BlockSpec((tm, tn), lambda i, j: (i, j)),
        2 * (tm * k * 2 + 2 * k * tn * 2 + tm * tn * 2) + 2 * tm * tn * 4 + 8 * MIB, "ffn_in", casts)


def _group_attention(q_heads, keys, vals, sinks, bias):
    rows = q_heads[0].shape[0]
    q = jnp.concatenate([qh.astype(BF16) for qh in q_heads], axis=0)
    s = lax.dot_general(q, keys, (((1,), (1,)), ((), ())), preferred_element_type=F32)
    probs, denoms = [], []
    for g, sk in enumerate(sinks):
        sg = s[g * rows:(g + 1) * rows]
        if bias is not None:
            sg = sg + bias
        sink = sk * LOG2E
        m = jnp.maximum(jnp.max(sg, axis=-1, keepdims=True), sink)
        p = jnp.exp2(sg - m)
        denoms.append(jnp.sum(p, axis=-1, keepdims=True) + jnp.exp2(sink - m))
        probs.append(p.astype(BF16))
    o = jnp.dot(jnp.concatenate(probs, axis=0), vals, preferred_element_type=F32)
    return [o[g * rows:(g + 1) * rows] / denoms[g] for g in range(len(q_heads))]


def _ctx_attn_kernel(sink_ref, q_ref, k_ref, v_ref, o_ref, *, layer):
    for hk in range(N_KV_HEADS):
        cols = slice(hk * HEAD_DIM, (hk + 1) * HEAD_DIM)
        heads = range(hk * GQA_GROUP, (hk + 1) * GQA_GROUP)
        q_heads = [q_ref[:, h * HEAD_DIM:(h + 1) * HEAD_DIM] * (SCALE * LOG2E) for h in heads]
        outs = _group_attention(q_heads, k_ref[:, cols].astype(BF16), v_ref[:, cols].astype(BF16),
                                [sink_ref[layer, h] for h in heads], None)
        for h, o in zip(heads, outs):
            o_ref[:, h * HEAD_DIM:(h + 1) * HEAD_DIM] = o.astype(o_ref.dtype)


def _ctx_attention(proj, sink, layer, n_seq, seq, out_rows):
    kb = ATTN_WIDTH // KV_WIDTH
    return pl.pallas_call(
        functools.partial(_ctx_attn_kernel, layer=layer),
        grid=(n_seq,),
        in_specs=[pl.BlockSpec(memory_space=pltpu.SMEM),
                  pl.BlockSpec((seq, ATTN_WIDTH), lambda b: (b, 0)),
                  pl.BlockSpec((seq, KV_WIDTH), lambda b: (b, kb)),
                  pl.BlockSpec((seq, KV_WIDTH), lambda b: (b, kb + 1))],
        out_specs=pl.BlockSpec((seq, ATTN_WIDTH), lambda b: (b, 0)),
        out_shape=jax.ShapeDtypeStruct((out_rows, ATTN_WIDTH), BF16),
        compiler_params=_cparams(("arbitrary",), 32 * MIB),
        name="ctx_attention",
    )(sink, proj, proj, proj)


def _rope(x, cos, sin_signed, lo_half):
    partner = jnp.where(lo_half, pltpu.roll(x, HEAD_DIM - AXIS_PAIRS, 1), pltpu.roll(x, AXIS_PAIRS, 1))
    return x * cos + partner * sin_signed


def _lat_attn_kernel(sink_ref, q_ref, kp_ref, kc_ref, kn_ref, vp_ref, vc_ref, vn_ref, ck_ref, cv_ref,
                     cq_ref, sq_ref, cp_ref, sp_ref, cn_ref, sn_ref, o_ref, *, layer, n_blocks):
    n = pl.program_id(1)
    past = ck_ref.shape[2]
    nk = 3 * BLOCK + past
    lane = lax.broadcasted_iota(jnp.int32, (BLOCK, HEAD_DIM), 1)
    lo_half = (lane % AXIS_DIM) < AXIS_PAIRS
    qi = lax.broadcasted_iota(jnp.int32, (BLOCK, nk), 0)
    kj = lax.broadcasted_iota(jnp.int32, (BLOCK, nk), 1)
    first_ok = jnp.where(n > 0, qi, BLOCK)
    last_ok = jnp.where(n < n_blocks - 1, qi, -1)
    bias_prev = jnp.where(kj >= first_ok, 0.0, NEG_INF)
    bias_next = jnp.where(kj - 2 * BLOCK <= last_ok, 0.0, NEG_INF)
    bias = jnp.where(kj < BLOCK, bias_prev, jnp.where(kj < 2 * BLOCK, 0.0, jnp.where(kj < 3 * BLOCK, bias_next, 0.0)))
    cq, sq = cq_ref[...], sq_ref[...]
    cq_scaled, sq_scaled = cq * (SCALE * LOG2E), sq * (SCALE * LOG2E)
    for hk in range(N_KV_HEADS):
        cols = slice(hk * HEAD_DIM, (hk + 1) * HEAD_DIM)
        heads = range(hk * GQA_GROUP, (hk + 1) * GQA_GROUP)
        keys = jnp.concatenate([
            _rope(kp_ref[:, cols], cp_ref[...], sp_ref[...], lo_half).astype(BF16),
            _rope(kc_ref[:, cols], cq, sq, lo_half).astype(BF16),
            _rope(kn_ref[:, cols], cn_ref[...], sn_ref[...], lo_half).astype(BF16),
            ck_ref[0, 0, :, cols].astype(BF16)], axis=0)
        vals = jnp.concatenate([vp_ref[:, cols].astype(BF16), vc_ref[:, cols].astype(BF16),
                                vn_ref[:, cols].astype(BF16), cv_ref[0, 0, :, cols].astype(BF16)], axis=0)
        q_heads = [_rope(q_ref[:, h * HEAD_DIM:(h + 1) * HEAD_DIM], cq_scaled, sq_scaled, lo_half) for h in heads]
        outs = _group_attention(q_heads, keys, vals, [sink_ref[layer, h] for h in heads], bias)
        for h, o in zip(heads, outs):
            o_ref[:, h * HEAD_DIM:(h + 1) * HEAD_DIM] = o.astype(o_ref.dtype)


def _lat_attention(proj, cache_k, cache_v, sink, rope_cos, rope_sin, layer, row0, n_seq, seq, into):
    nb = seq // BLOCK
    kb = ATTN_WIDTH // KV_WIDTH
    b0 = row0 // BLOCK
    past = cache_k.shape[2]

    def blk(shift):
        return lambda b, n: (b0 + b * nb + jnp.clip(n + shift, 0, nb - 1))

    def rows(shift, col):
        f = blk(shift)
        return lambda b, n: (f(b, n), col)

    def tab(shift):
        return pl.BlockSpec((BLOCK, HEAD_DIM), lambda b, n: (jnp.clip(n + shift, 0, nb - 1), 0))

    kv = lambda shift, col: pl.BlockSpec((BLOCK, KV_WIDTH), rows(shift, col))
    cache = pl.BlockSpec((1, 1, past, KV_WIDTH), lambda b, n: (b, layer, 0, 0))
    args = [sink, proj, proj, proj, proj, proj, proj, proj, cache_k, cache_v,
            rope_cos, rope_sin, rope_cos, rope_sin, rope_cos, rope_sin]
    specs = [pl.BlockSpec(memory_space=pltpu.SMEM),
             pl.BlockSpec((BLOCK, ATTN_WIDTH), rows(0, 0)),
             kv(-1, kb), kv(0, kb), kv(1, kb),
             kv(-1, kb + 1), kv(0, kb + 1), kv(1, kb + 1),
             cache, cache,
             tab(0), tab(0), tab(-1), tab(-1), tab(1), tab(1)]
    body, aliases = _reuse(functools.partial(_lat_attn_kernel, layer=layer, n_blocks=nb), args, specs, [into])
    return pl.pallas_call(
        body,
        grid=(n_seq, nb),
        in_specs=specs,
        out_specs=pl.BlockSpec((BLOCK, ATTN_WIDTH), lambda b, n: (b0 + b * nb + n, 0)),
        out_shape=jax.ShapeDtypeStruct(into.shape, into.dtype),
        input_output_aliases=aliases,
        compiler_params=_cparams(("arbitrary", "arbitrary"), 32 * MIB),
        name="latent_attention",
    )(*args)


def _conv3(u, w):
    n_rows = u.shape[0]
    row = lax.broadcasted_iota(jnp.int32, u.shape, 0)
    before = jnp.where(row == 0, 0.0, pltpu.roll(u, 1, 0))
    after = jnp.where(row == n_rows - 1, 0.0, pltpu.roll(u, n_rows - 1, 0))
    return before * w[0:1] + u * w[1:2] + after * w[2:3]


def _short_conv_kernel(b_ref, c_ref, x_ref, w_ref, o_ref):
    o_ref[...] = (b_ref[...] * _conv3(c_ref[...] * x_ref[...], w_ref[0])).astype(o_ref.dtype)


def _short_conv(proj, conv_w, layer, col0, row0, n_seq, seq, tc, out_rows, into=None):
    cw = conv_w.shape[2]
    tc = _tile(cw, tc)
    nc = cw // tc
    cb, rb = col0 // tc, row0 // seq

    def part(p):
        return pl.BlockSpec((seq, tc), lambda s, c: (rb + s, cb + p * nc + c))

    args = [proj, proj, proj, conv_w]
    specs = [part(0), part(1), part(2), pl.BlockSpec((1, 3, tc), lambda s, c: (layer, 0, c))]
    body, aliases = _reuse(_short_conv_kernel, args, specs, [into] if into is not None else None)
    return pl.pallas_call(
        body,
        grid=(n_seq, nc),
        in_specs=specs,
        out_specs=pl.BlockSpec((seq, tc), lambda s, c: (rb + s, c)),
        out_shape=jax.ShapeDtypeStruct((out_rows, cw), BF16),
        input_output_aliases=aliases,
        compiler_params=_cparams(("arbitrary", "arbitrary"), 2 * seq * tc * 14 + 6 * seq * tc * 4 + 8 * MIB),
        name="short_conv",
    )(*args)


def _hyena_kernel(x0_ref, x1_ref, v_ref, w0_ref, w1_ref, wv_ref, skip_ref, kre_ref, kim_ref,
                  c_ref, s_ref, ct_ref, st_ref, o_ref, g32_ref, g16_ref, acc_ref, *, n_k, scale):
    kk = pl.program_id(2)

    @pl.when(kk == 0)
    def _():
        g = _conv3(x1_ref[...], w1_ref[0]) * _conv3(v_ref[...], wv_ref[0])
        g32_ref[...] = g
        g16_ref[...] = g.astype(BF16)
        acc_ref[...] = jnp.zeros_like(acc_ref)

    g16 = g16_ref[...]
    u_c = jnp.dot(c_ref[...], g16, preferred_element_type=F32)
    u_s = jnp.dot(s_ref[...], g16, preferred_element_type=F32)
    kre, kim = kre_ref[0], kim_ref[0]
    y_re = u_c * kre + u_s * kim
    y_im = u_c * kim - u_s * kre
    acc_ref[...] += (jnp.dot(ct_ref[...], y_re.astype(BF16), preferred_element_type=F32)
                     - jnp.dot(st_ref[...], y_im.astype(BF16), preferred_element_type=F32))

    @pl.when(kk == n_k - 1)
    def _():
        y = acc_ref[...] * scale + skip_ref[0] * g32_ref[...]
        o_ref[...] = (_conv3(x0_ref[...], w0_ref[0]) * y).astype(o_ref.dtype)


def _hyena(proj, hyena_conv_w, hyena_skip, k_re, k_im, tabs, layer, col0, row0, n_seq, seq, tc, kc, out_rows,
           into=None):
    hw = hyena_skip.shape[2]
    tc, kc = _tile(hw, tc), _tile(seq, kc)
    nc, n_k = hw // tc, seq // kc
    cb, rb = col0 // tc, row0 // seq
    cos_b, sin_b, cos_t, sin_t = tabs

    def part(p):
        return pl.BlockSpec((seq, tc), lambda s, c, k: (rb + s, cb + p * nc + c))

    def wpart(p):
        return pl.BlockSpec((1, 3, tc), lambda s, c, k: (layer, 0, p * nc + c))

    spec_k = pl.BlockSpec((1, kc, tc), lambda s, c, k: (layer, k, c))
    fwd = pl.BlockSpec((kc, seq), lambda s, c, k: (k, 0))
    inv = pl.BlockSpec((seq, kc), lambda s, c, k: (0, k))
    args = [proj, proj, proj, hyena_conv_w, hyena_conv_w, hyena_conv_w, hyena_skip, k_re, k_im,
            cos_b, sin_b, cos_t, sin_t]
    specs = [part(0), part(1), part(2), wpart(0), wpart(1), wpart(2),
             pl.BlockSpec((1, 1, tc), lambda s, c, k: (layer, 0, c)),
             spec_k, spec_k, fwd, fwd, inv, inv]
    body, aliases = _reuse(functools.partial(_hyena_kernel, n_k=n_k, scale=1.0 / seq), args, specs,
                           [into] if into is not None else None)
    return pl.pallas_call(
        body,
        grid=(n_seq, nc, n_k),
        in_specs=specs,
        out_specs=pl.BlockSpec((seq, tc), lambda s, c, k: (rb + s, c)),
        out_shape=jax.ShapeDtypeStruct((out_rows, hw), BF16),
        input_output_aliases=aliases,
        scratch_shapes=[pltpu.VMEM((seq, tc), F32), pltpu.VMEM((seq, tc), BF16), pltpu.VMEM((seq, tc), F32)],
        compiler_params=_cparams(("arbitrary", "arbitrary", "arbitrary"),
                                 2 * (3 * seq * tc * 4 + seq * tc * 2 + 2 * kc * tc * 4 + 4 * kc * seq * 2)
                                 + seq * tc * 10 + 6 * seq * tc * 4 + 8 * MIB),
        name="hyena",
    )(*args)


def _split_bf16(x):
    hi = x.astype(BF16)
    return hi, (x - hi.astype(F32)).astype(BF16)


def _dot_split(a_hi, a_lo, b_hi, b_lo):
    return (jnp.dot(a_hi, b_hi, preferred_element_type=F32) + jnp.dot(a_hi, b_lo, preferred_element_type=F32)
            + jnp.dot(a_lo, b_hi, preferred_element_type=F32))


def _filter_kernel(z_ref, w1_ref, b1_ref, fr_ref, w2_ref, b2_ref, w3f_ref, w3b_ref, decay_ref,
                   ch_ref, cl_ref, sh_ref, sl_ref, kre_ref, kim_ref, hph_ref, hpl_ref, hmh_ref, hml_ref):
    @pl.when(pl.program_id(2) == 0)
    def _():
        fr = fr_ref[0]
        hdn = jnp.sin(fr * (jnp.dot(z_ref[...], w1_ref[0], precision=HI, preferred_element_type=F32) + b1_ref[0]))
        hdn = jnp.sin(fr * (jnp.dot(hdn, w2_ref[0], precision=HI, preferred_element_type=F32) + b2_ref[0]))
        decay = decay_ref[...]
        h_fwd = jnp.dot(hdn, w3f_ref[0], precision=HI, preferred_element_type=F32) * decay
        h_bwd = jnp.dot(hdn, w3b_ref[0], precision=HI, preferred_element_type=F32) * decay
        row = lax.broadcasted_iota(jnp.int32, h_bwd.shape, 0)
        h_bwd = jnp.where(row == 0, 0.0, h_bwd)
        hph_ref[...], hpl_ref[...] = _split_bf16(h_fwd + h_bwd)
        hmh_ref[...], hml_ref[...] = _split_bf16(h_fwd - h_bwd)

    kre_ref[0] = _dot_split(ch_ref[...], cl_ref[...], hph_ref[...], hpl_ref[...])
    kim_ref[0] = -_dot_split(sh_ref[...], sl_ref[...], hmh_ref[...], hml_ref[...])


def _filter_spectra(z, w1, b1, fr, w2, b2, w3, decay, tabs, tc, kc):
    depth = w1.shape[0]
    seq, hw = decay.shape
    tc, kc = _tile(hw, tc), _tile(seq, kc)
    nc, n_k = hw // tc, seq // kc
    small = lambda shape: pl.BlockSpec((1,) + shape, lambda l, c, k: (l, 0, 0))
    out = pl.BlockSpec((1, kc, tc), lambda l, c, k: (l, k, c))
    tab = pl.BlockSpec((kc, seq), lambda l, c, k: (k, 0))
    return pl.pallas_call(
        _filter_kernel,
        grid=(depth, nc, n_k),
        in_specs=[pl.BlockSpec((seq, FILT_PAD), lambda l, c, k: (0, 0)),
                  small((FILT_PAD, FILT_PAD)), small((1, FILT_PAD)), small((1, FILT_PAD)),
                  small((FILT_PAD, FILT_PAD)), small((1, FILT_PAD)),
                  pl.BlockSpec((1, FILT_PAD, tc), lambda l, c, k: (l, 0, c)),
                  pl.BlockSpec((1, FILT_PAD, tc), lambda l, c, k: (l, 0, nc + c)),
                  pl.BlockSpec((seq, tc), lambda l, c, k: (0, c)),
                  tab, tab, tab, tab],
        out_specs=[out, out],
        out_shape=[jax.ShapeDtypeStruct((depth, seq, hw), F32)] * 2,
        scratch_shapes=[pltpu.VMEM((seq, tc), BF16)] * 4,
        compiler_params=_cparams(("arbitrary", "arbitrary", "arbitrary"),
                                 2 * (seq * tc * 4 + 4 * kc * seq * 2 + 2 * kc * tc * 4 + seq * FILT_PAD * 4)
                                 + 6 * seq * tc * 4 + 8 * MIB),
        name="filter_spectra",
    )(z, w1, b1, fr, w2, b2, w3, w3, decay, *tabs)


def _rope_tables(seq):
    rows = seq // GRID_W
    t_row = jnp.repeat(jnp.arange(rows, dtype=F32), GRID_W)
    t_col = jnp.tile(jnp.arange(GRID_W, dtype=F32), rows)
    inv = ROPE_BASE ** (-jnp.arange(0, AXIS_DIM, 2, dtype=F32) / AXIS_DIM)
    c_r, s_r = jnp.cos(t_row[:, None] * inv), jnp.sin(t_row[:, None] * inv)
    c_c, s_c = jnp.cos(t_col[:, None] * inv), jnp.sin(t_col[:, None] * inv)
    cos = jnp.concatenate([c_r, c_r, c_c, c_c], axis=-1)
    sin_signed = jnp.concatenate([-s_r, s_r, -s_c, s_c], axis=-1)
    return cos, sin_signed


def _filter_features(seq):
    pos = jnp.arange(seq, dtype=F32)
    t = jnp.linspace(0.0, 1.0, seq, dtype=F32)
    bands = (FILTER_EMB - 1) // 2
    f = jnp.linspace(1e-4, bands - 1, bands, dtype=F32)
    ang = 2.0 * math.pi * pos[:, None] * f[None, :] / seq
    z = jnp.concatenate([t[:, None], jnp.cos(ang), -jnp.sin(ang)], axis=-1)
    return jnp.pad(z, ((0, 0), (0, FILT_PAD - FILTER_EMB)))


def _filter_decay(seq, width):
    t = jnp.linspace(0.0, 1.0, seq, dtype=F32)
    deltas = jnp.abs(jnp.linspace(math.log(HYENA_TARGET) / SLOW_DECAY_PCT,
                                  math.log(HYENA_TARGET) / FAST_DECAY_PCT, width, dtype=F32))
    return jnp.exp(-t[:, None] * deltas[None, :])


def _odd_dft_tables(seq):
    k = jnp.arange(seq, dtype=jnp.int32)[:, None]
    t = jnp.arange(seq, dtype=jnp.int32)[None, :]
    phase = ((2 * k + 1) * t) % (4 * seq)
    ang = phase.astype(F32) * (math.pi / (2 * seq))
    return jnp.cos(ang), jnp.sin(ang)


def kernel(x_prompt, x_sample, cache_k, cache_v, c, c_ctx, norm_g, w_mod, b_mod, w_in, attn_sink, conv_w, hyena_conv_w,
           filt_w1, filt_b1, filt_freq, filt_w2, filt_b2, filt_w3, hyena_skip, w_branch, w_gate, b_gate, w_o, w_ffn_in,
           w_ffn_out):
    n_ctx, s_ctx, d = x_prompt.shape
    n_lat, s_lat, _ = x_sample.shape
    depth = w_in.shape[0]
    cw = conv_w.shape[2]
    hw = hyena_skip.shape[1]
    d_ff = w_ffn_out.shape[1]
    past = cache_k.shape[2]
    m_ctx, m_lat = n_ctx * s_ctx, n_lat * s_lat
    m = m_ctx + m_lat
    tm_row = s_ctx
    assert s_lat % tm_row == 0 and m_ctx % s_lat == 0 and n_lat + 1 <= MOD_ROWS
    col_conv = ATTN_WIDTH + 2 * KV_WIDTH
    col_hy = col_conv + 3 * cw

    def seq_of_tile(i):
        n_ctx_tiles, per_lat = m_ctx // tm_row, s_lat // tm_row
        return jnp.where(i < n_ctx_tiles, 0, 1 + (i - n_ctx_tiles) // per_lat)

    w_in_b = w_in[:1].astype(BF16)
    w_gate_b = w_gate[:1].astype(BF16)

    cvecs = jnp.concatenate([c_ctx[None], c, jnp.zeros((MOD_ROWS - 1 - n_lat, d), F32)], axis=0)
    mod_tab = _modulations(cvecs, w_mod, b_mod).reshape(depth * MOD_ROWS * N_MOD, 1, d)
    norm_tab = norm_g.reshape(depth * 4, 1, d)

    fp = FILT_PAD - filt_w1.shape[2]
    f_w1 = jnp.pad(filt_w1, ((0, 0), (0, FILT_PAD - FILTER_EMB), (0, fp)))
    f_b1 = jnp.pad(filt_b1, ((0, 0), (0, fp)))[:, None]
    f_fr = jnp.pad(filt_freq, ((0, 0), (0, fp)))[:, None]
    f_w2 = jnp.pad(filt_w2, ((0, 0), (0, fp), (0, fp)))
    f_b2 = jnp.pad(filt_b2, ((0, 0), (0, fp)))[:, None]
    f_w3 = jnp.pad(filt_w3, ((0, 0), (0, fp), (0, 0)))
    spectra, dft = {}, {}
    for seq in sorted({s_ctx, s_lat}):
        cos32, sin32 = _odd_dft_tables(seq)
        cos_b, sin_b = cos32.astype(BF16), sin32.astype(BF16)
        cos_lo, sin_lo = (cos32 - cos_b.astype(F32)).astype(BF16), (sin32 - sin_b.astype(F32)).astype(BF16)
        spectra[seq] = _filter_spectra(_filter_features(seq), f_w1, f_b1, f_fr, f_w2, f_b2, f_w3,
                                       _filter_decay(seq, hw), (cos_b, cos_lo, sin_b, sin_lo), 512, 256)
        dft[seq] = (cos_b, sin_b, cos_b.T, sin_b.T)

    rope_cos, rope_sin = _rope_tables(s_lat)
    cache_k2 = cache_k.reshape(n_lat, depth, past, KV_WIDTH)
    cache_v2 = cache_v.reshape(n_lat, depth, past, KV_WIDTH)
    b_gate3 = b_gate.reshape(depth, 1, -1)
    skip3 = hyena_skip.reshape(depth, 1, hw)

    groups = ((0, m_ctx), (m_ctx, m_lat))

    def res_stage(x_parts, y, split_out, **kw):
        if len(x_parts) == 1 and not split_out:
            return _resnorm(x_parts[0][0], y, norm_tab, mod_tab, seq_of_tile, tm_row, rows=(0, m), **kw)
        outs, per_group = None, []
        for gi, (g0, n_rows) in enumerate(groups):
            xa, xb = x_parts[gi] if len(x_parts) > 1 else x_parts[0]
            if split_out:
                per_group.append(_resnorm(xa, y, norm_tab, mod_tab, seq_of_tile, tm_row, rows=(g0, n_rows), x_base=xb,
                                          out_base=g0, **kw))
            else:
                outs = _resnorm(xa, y, norm_tab, mod_tab, seq_of_tile, tm_row, rows=(g0, n_rows), x_base=xb,
                                out_rows=m, into=outs, **kw)
        return per_group if split_out else outs

    x_parts = [(x_prompt.reshape(m_ctx, d), 0), (x_sample.reshape(m_lat, d), m_ctx)]
    (h,) = res_stage(x_parts, None, False, h_norm=0, h_scale=(0, 1), h_shift=(0, 0))
    new_k, new_v = [], []
    for l in range(depth):
        proj, (w_branch_b,) = _matmul(h, w_in_b, 0, F32, 1024, 1024, [(w_branch, l)])
        gates, (w_o_b,) = _gates(h, w_gate_b, b_gate3, 0, l, 1024, 1024, [(w_o, l)])
        new_k.append(proj[:m_ctx, ATTN_WIDTH:ATTN_WIDTH + KV_WIDTH].reshape(n_ctx, s_ctx, N_KV_HEADS, HEAD_DIM))
        new_v.append(proj[:m_ctx, ATTN_WIDTH + KV_WIDTH:col_conv].reshape(n_ctx, s_ctx, N_KV_HEADS, HEAD_DIM))

        o_attn = _ctx_attention(proj, attn_sink, l, n_ctx, s_ctx, m)
        o_attn = _lat_attention(proj, cache_k2, cache_v2, attn_sink, rope_cos, rope_sin, l, m_ctx, n_lat, s_lat, o_attn)
        o_conv = _short_conv(proj, conv_w, l, col_conv, 0, n_ctx, s_ctx, 512, m)
        o_conv = _short_conv(proj, conv_w, l, col_conv, m_ctx, n_lat, s_lat, 256, m, o_conv)
        o_hy = _hyena(proj, hyena_conv_w, skip3, *spectra[s_ctx], dft[s_ctx], l, col_hy, 0, n_ctx, s_ctx, 512, 512, m)
        o_hy = _hyena(proj, hyena_conv_w, skip3, *spectra[s_lat], dft[s_lat], l, col_hy, m_ctx, n_lat, s_lat, 256, 512,
                      m, o_hy)

        merged, (w_ffn_in_b,) = _merge(o_attn, o_conv, o_hy, w_branch_b[None], gates, 0, 1024, 512, [(w_ffn_in, l)])
        mix, _ = _matmul(merged, w_o_b[None], 0, BF16, 1024, 1024)
        x, h = res_stage(x_parts, mix, False, y_norm=l * 4 + 1, gate=(l, 2),
                         h_norm=l * 4 + 2, h_scale=(l, 4), h_shift=(l, 3))
        x_parts = [(x, 0)]
        next_casts = [(w_in, l + 1), (w_gate, l + 1)] if l + 1 < depth else []
        hidden, (w_ffn_out_b, *next_w) = _ffn_in(h, w_ffn_in_b[None], 0, 1024, 2, [(w_ffn_out, l)] + next_casts)
        if next_w:
            w_in_b, w_gate_b = next_w[0][None], next_w[1][None]
        f, _ = _matmul(hidden, w_ffn_out_b[None], 0, BF16, 512, 512)
        if l + 1 < depth:
            x, h = res_stage(x_parts, f, False, y_norm=l * 4 + 3, gate=(l, 5),
                             h_norm=(l + 1) * 4, h_scale=(l + 1, 1), h_shift=(l + 1, 0))
            x_parts = [(x, 0)]
        else:
            (y_prompt,), (y_sample,) = res_stage(x_parts, f, True, y_norm=l * 4 + 3, gate=(l, 5))

    return (y_prompt.reshape(n_ctx, s_ctx, d), y_sample.reshape(n_lat, s_lat, d),
            jnp.stack(new_k, axis=1), jnp.stack(new_v, axis=1))
```

```python
import functools
import math

import jax
import jax.numpy as jnp
from jax import lax
from jax.experimental import pallas as pl
from jax.experimental.pallas import tpu as pltpu

F32 = jnp.float32
BF16 = jnp.bfloat16

HEAD_DIM = 128
N_HEADS = 16
N_KV_HEADS = 4
GQA_GROUP = N_HEADS // N_KV_HEADS
ATTN_WIDTH = N_HEADS * HEAD_DIM
KV_WIDTH = N_KV_HEADS * HEAD_DIM
WINDOW = 128
BLOCK = 128
GRID_W = 64
ROPE_BASE = 10000.0
AXIS_DIM = HEAD_DIM // 2
AXIS_PAIRS = AXIS_DIM // 2
FILTER_EMB = 33
HYENA_TARGET = 1e-2
FAST_DECAY_PCT = 0.3
SLOW_DECAY_PCT = 1.5
EPS = 1e-6
SCALE = HEAD_DIM ** -0.5
LOG2E = math.log2(math.e)
NEG_INF = -1e30
N_MOD = 6
MOD_ROWS = 8
FILT_PAD = 128

LANE = 128
MXU_COLS = 256
BF16_ROWS = 16
MIB = 1024 * 1024
VMEM_CAP_BYTES = 56 * MIB
HI = lax.Precision.HIGHEST


def _cparams(semantics, vmem_bytes):
    return pltpu.CompilerParams(dimension_semantics=semantics,
                                vmem_limit_bytes=int(min(max(vmem_bytes, 16 * MIB), VMEM_CAP_BYTES)))


def _tile(n, pref):
    if n <= pref:
        return n
    t = (pref // LANE) * LANE
    while t > LANE and n % t:
        t -= LANE
    assert n % t == 0, (n, pref)
    return t


def _skip_alias(body, n_in, n_alias, *refs):
    body(*refs[:n_in], *refs[n_in + n_alias:])


def _reuse(body, args, specs, into):
    if not into:
        return body, {}
    n_in = len(args)
    aliases = {n_in + k: k for k in range(len(into))}
    args += list(into)
    specs += [pl.BlockSpec(memory_space=pl.ANY)] * len(into)
    return functools.partial(_skip_alias, body, n_in, len(into)), aliases


def _mods_kernel(c_ref, w_ref, b_ref, o_ref):
    c = c_ref[...]
    a = (c * jax.nn.sigmoid(c)).astype(BF16)
    o_ref[0] = jnp.dot(a, w_ref[0].astype(BF16), preferred_element_type=F32) + b_ref[0]


def _modulations(cvecs, w_mod, b_mod):
    depth, d, n = w_mod.shape
    tn = _tile(n, 512)
    return pl.pallas_call(
        _mods_kernel,
        grid=(depth, n // tn),
        in_specs=[
            pl.BlockSpec((MOD_ROWS, d), lambda l, j: (0, 0)),
            pl.BlockSpec((1, d, tn), lambda l, j: (l, 0, j)),
            pl.BlockSpec((1, 1, tn), lambda l, j: (l, 0, j)),
        ],
        out_specs=pl.BlockSpec((1, MOD_ROWS, tn), lambda l, j: (l, 0, j)),
        out_shape=jax.ShapeDtypeStruct((depth, MOD_ROWS, n), F32),
        compiler_params=_cparams(("arbitrary", "arbitrary"), 2 * d * tn * 4 + 8 * MIB),
        name="modulations",
    )(cvecs, w_mod, b_mod.reshape(depth, 1, n))


def _rms(x):
    return x * lax.rsqrt(jnp.mean(x * x, axis=-1, keepdims=True) + EPS)


def _resnorm_kernel(*refs, has_y, has_h):
    it = iter(refs)
    x_ref = next(it)
    if has_y:
        y_ref, gny_ref, gate_ref = next(it), next(it), next(it)
    if has_h:
        gnx_ref, sc_ref, sh_ref = next(it), next(it), next(it)
    if has_y:
        xo_ref = next(it)
    if has_h:
        h_ref = next(it)
    x = x_ref[...]
    if has_y:
        x = x + gate_ref[0] * (_rms(y_ref[...].astype(F32)) * gny_ref[0])
        xo_ref[...] = x
    if has_h:
        h_ref[...] = ((_rms(x) * gnx_ref[0]) * (1.0 + sc_ref[0]) + sh_ref[0]).astype(h_ref.dtype)


def _resnorm(x, y, norm_tab, mod_tab, seq_of_tile, tm, *, rows, x_base=0, out_base=0, out_rows=None, into=None,
             y_norm=None, gate=None, h_norm=None, h_scale=None, h_shift=None):
    g0, n_rows = rows
    d = x.shape[1]
    out_rows = n_rows if out_rows is None else out_rows
    has_y, has_h = y is not None, h_norm is not None
    assert g0 % tm == 0 and n_rows % tm == 0 and x_base % tm == 0 and out_base % tm == 0
    xt, gt, ot = (g0 - x_base) // tm, g0 // tm, (g0 - out_base) // tm
    x_row = pl.BlockSpec((tm, d), lambda i: (i + xt, 0))
    y_row = pl.BlockSpec((tm, d), lambda i: (i + gt, 0))
    o_row = pl.BlockSpec((tm, d), lambda i: (i + ot, 0))

    def tab(idx):
        return pl.BlockSpec((1, 1, d), lambda i: (idx, 0, 0))

    def mod(layer_which):
        layer, which = layer_which
        return pl.BlockSpec((1, 1, d), lambda i: ((layer * MOD_ROWS + seq_of_tile(i + gt)) * N_MOD + which, 0, 0))

    args, specs = [x], [x_row]
    if has_y:
        args += [y, norm_tab, mod_tab]
        specs += [y_row, tab(y_norm), mod(gate)]
    if has_h:
        args += [norm_tab, mod_tab, mod_tab]
        specs += [tab(h_norm), mod(h_scale), mod(h_shift)]
    out_shape, out_specs = [], []
    if has_y:
        out_shape.append(jax.ShapeDtypeStruct((out_rows, d), F32))
        out_specs.append(o_row)
    if has_h:
        out_shape.append(jax.ShapeDtypeStruct((out_rows, d), BF16))
        out_specs.append(o_row)
    body, aliases = _reuse(functools.partial(_resnorm_kernel, has_y=has_y, has_h=has_h), args, specs, into)
    return pl.pallas_call(
        body,
        grid=(n_rows // tm,),
        in_specs=specs,
        out_specs=out_specs,
        out_shape=out_shape,
        input_output_aliases=aliases,
        compiler_params=_cparams(("arbitrary",), 2 * tm * d * 14 + 8 * MIB),
        name="resnorm",
    )(*args)


def _with_casts(body, n_in, n_cast, *refs):
    srcs = refs[n_in:n_in + n_cast]
    dsts = refs[n_in + n_cast + 1:n_in + 2 * n_cast + 1]
    body(*refs[:n_in], refs[n_in + n_cast], *refs[n_in + 2 * n_cast + 1:])
    for src, dst in zip(srcs, dsts):
        dst[...] = src[...].astype(dst.dtype)


def _dense_call(body, args, specs, grid, out_shape, out_spec, vmem_bytes, name, casts=(), into=None):
    nj = grid[1]
    steps = grid[0] * nj
    n_in = len(args)
    args, specs = list(args), list(specs)
    out_shapes, out_specs = [out_shape], [out_spec]
    for w, layer in casts:
        k, cols = w.shape[1], w.shape[2]
        rows = BF16_ROWS
        while k % rows or k // rows > steps:
            rows += BF16_ROWS
        n_blocks = k // rows
        blk = lambda i, j, n_blocks=n_blocks: jnp.minimum(i * nj + j, n_blocks - 1)
        args.append(w.reshape(-1, cols))
        specs.append(pl.BlockSpec((rows, cols), lambda i, j, blk=blk, first=layer * n_blocks: (first + blk(i, j), 0)))
        out_shapes.append(jax.ShapeDtypeStruct((k, cols), BF16))
        out_specs.append(pl.BlockSpec((rows, cols), lambda i, j, blk=blk: (blk(i, j), 0)))
        vmem_bytes += 2 * rows * cols * 6
    kernel = functools.partial(_with_casts, body, n_in, len(casts))
    kernel, aliases = _reuse(kernel, args, specs, [into] if into is not None else None)
    outs = pl.pallas_call(
        kernel, grid=grid, in_specs=specs, out_specs=out_specs, out_shape=out_shapes, input_output_aliases=aliases,
        compiler_params=_cparams(("arbitrary", "arbitrary"), vmem_bytes), name=name,
    )(*args)
    return outs[0], list(outs[1:])


def _mm_kernel(a_ref, w_ref, o_ref):
    o_ref[...] = jnp.dot(a_ref[...], w_ref[0], preferred_element_type=F32).astype(o_ref.dtype)


def _matmul(a, w, layer, out_dtype, tm, tn, casts=()):
    m, k = a.shape
    n = w.shape[2]
    tm, tn = _tile(m, tm), _tile(n, tn)
    ob = jnp.dtype(out_dtype).itemsize
    return _dense_call(
        _mm_kernel, [a, w],
        [pl.BlockSpec((tm, k), lambda i, j: (i, 0)), pl.BlockSpec((1, k, tn), lambda i, j: (layer, 0, j))],
        (m // tm, n // tn), jax.ShapeDtypeStruct((m, n), out_dtype), pl.BlockSpec((tm, tn), lambda i, j: (i, j)),
        2 * (tm * k * 2 + k * tn * 2 + tm * tn * ob) + 2 * tm * tn * 4 + 8 * MIB, "matmul", casts)


def _sigmoid(x):
    return 0.5 * jnp.tanh(0.5 * x) + 0.5


def _gate_kernel(a_ref, w_ref, b_ref, o_ref):
    acc = jnp.dot(a_ref[...], w_ref[0], preferred_element_type=F32) + b_ref[0]
    o_ref[...] = _sigmoid(acc).astype(o_ref.dtype)


def _gates(h, w_gate, b_gate, layer, b_layer, tm, tn, casts=()):
    m, k = h.shape
    n = w_gate.shape[2]
    tm, tn = _tile(m, tm), _tile(n, tn)
    return _dense_call(
        _gate_kernel, [h, w_gate, b_gate],
        [pl.BlockSpec((tm, k), lambda i, j: (i, 0)),
         pl.BlockSpec((1, k, tn), lambda i, j: (layer, 0, j)),
         pl.BlockSpec((1, 1, tn), lambda i, j: (b_layer, 0, j))],
        (m // tm, n // tn), jax.ShapeDtypeStruct((m, n), BF16), pl.BlockSpec((tm, tn), lambda i, j: (i, j)),
        2 * (tm * k * 2 + k * tn * 2 + tm * tn * 2) + 2 * tm * tn * 4 + 8 * MIB, "gates", casts)


def _merge_kernel(oa_ref, oc_ref, oh_ref, wa_ref, wc_ref, wh_ref, ga_ref, gc_ref, gh_ref, o_ref):
    ba = jnp.dot(oa_ref[...], wa_ref[0], preferred_element_type=F32)
    bc = jnp.dot(oc_ref[...], wc_ref[0], preferred_element_type=F32)
    bh = jnp.dot(oh_ref[...], wh_ref[0], preferred_element_type=F32)
    merged = ga_ref[...].astype(F32) * ba + gc_ref[...].astype(F32) * bc + gh_ref[...].astype(F32) * bh
    o_ref[...] = merged.astype(o_ref.dtype)


def _merge(o_attn, o_conv, o_hy, w_branch, gates, layer, tm, tn, casts=()):
    m = o_attn.shape[0]
    d = w_branch.shape[2]
    cw = o_conv.shape[1]
    assert ATTN_WIDTH % cw == 0
    tm, tn = _tile(m, tm), _tile(d, tn)
    nj = d // tn
    conv_blk = ATTN_WIDTH // cw
    return _dense_call(
        _merge_kernel, [o_attn, o_conv, o_hy, w_branch, w_branch, w_branch, gates, gates, gates],
        [pl.BlockSpec((tm, ATTN_WIDTH), lambda i, j: (i, 0)),
         pl.BlockSpec((tm, cw), lambda i, j: (i, 0)),
         pl.BlockSpec((tm, cw), lambda i, j: (i, 0)),
         pl.BlockSpec((1, ATTN_WIDTH, tn), lambda i, j: (layer, 0, j)),
         pl.BlockSpec((1, cw, tn), lambda i, j: (layer, conv_blk, j)),
         pl.BlockSpec((1, cw, tn), lambda i, j: (layer, conv_blk + 1, j)),
         pl.BlockSpec((tm, tn), lambda i, j: (i, j)),
         pl.BlockSpec((tm, tn), lambda i, j: (i, nj + j)),
         pl.BlockSpec((tm, tn), lambda i, j: (i, 2 * nj + j))],
        (m // tm, nj), jax.ShapeDtypeStruct((m, d), BF16), pl.BlockSpec((tm, tn), lambda i, j: (i, j)),
        2 * (tm * (ATTN_WIDTH + 2 * cw) * 2 + (ATTN_WIDTH + 2 * cw) * tn * 2 + 4 * tm * tn * 2)
        + 3 * tm * tn * 4 + 8 * MIB, "merge", casts)


def _ffn_in_kernel(h_ref, *refs, n_sub):
    w_refs, o_ref = refs[:-1], refs[-1]
    h = h_ref[...]
    for s in range(n_sub):
        a = jnp.dot(h, w_refs[s][0], preferred_element_type=F32)
        b = jnp.dot(h, w_refs[n_sub + s][0], preferred_element_type=F32)
        o_ref[:, s * MXU_COLS:(s + 1) * MXU_COLS] = (a * _sigmoid(a) * b).astype(o_ref.dtype)


def _ffn_in(h, w_ffn_in, layer, tm, n_sub, casts=()):
    m, k = h.shape
    ff = w_ffn_in.shape[2] // 2
    assert ff % MXU_COLS == 0
    tm = _tile(m, tm)
    nb = ff // MXU_COLS

    def call(first, n_tiles, width, casts, into):
        assert first % width == 0
        tn = width * MXU_COLS
        w_spec = lambda half, s: pl.BlockSpec(
            (1, k, MXU_COLS), lambda i, j: (layer, 0, half * nb + first + width * j + s))
        return _dense_call(
            functools.partial(_ffn_in_kernel, n_sub=width), [h] + [w_ffn_in] * (2 * width),
            [pl.BlockSpec((tm, k), lambda i, j: (i, 0))]
            + [w_spec(0, s) for s in range(width)] + [w_spec(1, s) for s in range(width)],
            (m // tm, n_tiles), jax.ShapeDtypeStruct((m, ff), BF16),
            pl.BlockSpec((tm, tn), lambda i, j: (i, first // width + j)),
            2 * (tm * k * 2 + 2 * k * tn * 2 + tm * tn * 2) + 2 * tm * tn * 4 + 8 * MIB, "ffn_in", casts, into)

    hidden, cast_out = call(0, nb // n_sub, n_sub, casts, None)
    if nb % n_sub:
        hidden, _ = call(nb - nb % n_sub, 1, nb % n_sub, (), hidden)
    return hidden, cast_out


def _group_attention(q_heads, keys, vals, sinks, bias):
    rows = q_heads[0].shape[0]
    q = jnp.concatenate([qh.astype(BF16) for qh in q_heads], axis=0)
    s = lax.dot_general(q, keys, (((1,), (1,)), ((), ())), preferred_element_type=F32)
    probs, denoms = [], []
    for g, sk in enumerate(sinks):
        sg = s[g * rows:(g + 1) * rows]
        if bias is not None:
            sg = sg + bias
        sink = sk * LOG2E
        m = jnp.maximum(jnp.max(sg, axis=-1, keepdims=True), sink)
        p = jnp.exp2(sg - m)
        denoms.append(jnp.sum(p, axis=-1, keepdims=True) + jnp.exp2(sink - m))
        probs.append(p.astype(BF16))
    o = jnp.dot(jnp.concatenate(probs, axis=0), vals, preferred_element_type=F32)
    return [o[g * rows:(g + 1) * rows] / denoms[g] for g in range(len(q_heads))]


def _ctx_attn_kernel(sink_ref, q_ref, k_ref, v_ref, o_ref, *, layer):
    for hk in range(N_KV_HEADS):
        cols = slice(hk * HEAD_DIM, (hk + 1) * HEAD_DIM)
        heads = range(hk * GQA_GROUP, (hk + 1) * GQA_GROUP)
        q_heads = [q_ref[:, h * HEAD_DIM:(h + 1) * HEAD_DIM] * (SCALE * LOG2E) for h in heads]
        outs = _group_attention(q_heads, k_ref[:, cols].astype(BF16), v_ref[:, cols].astype(BF16),
                                [sink_ref[layer, h] for h in heads], None)
        for h, o in zip(heads, outs):
            o_ref[:, h * HEAD_DIM:(h + 1) * HEAD_DIM] = o.astype(o_ref.dtype)


def _ctx_attention(proj, sink, layer, n_seq, seq, out_rows):
    kb = ATTN_WIDTH // KV_WIDTH
    return pl.pallas_call(
        functools.partial(_ctx_attn_kernel, layer=layer),
        grid=(n_seq,),
        in_specs=[pl.BlockSpec(memory_space=pltpu.SMEM),
                  pl.BlockSpec((seq, ATTN_WIDTH), lambda b: (b, 0)),
                  pl.BlockSpec((seq, KV_WIDTH), lambda b: (b, kb)),
                  pl.BlockSpec((seq, KV_WIDTH), lambda b: (b, kb + 1))],
        out_specs=pl.BlockSpec((seq, ATTN_WIDTH), lambda b: (b, 0)),
        out_shape=jax.ShapeDtypeStruct((out_rows, ATTN_WIDTH), BF16),
        compiler_params=_cparams(("arbitrary",), 32 * MIB),
        name="ctx_attention",
    )(sink, proj, proj, proj)


def _rope(x, cos, sin_signed, lo_half):
    partner = jnp.where(lo_half, pltpu.roll(x, HEAD_DIM - AXIS_PAIRS, 1), pltpu.roll(x, AXIS_PAIRS, 1))
    return x * cos + partner * sin_signed


def _lat_attn_kernel(sink_ref, q_ref, kp_ref, kc_ref, kn_ref, vp_ref, vc_ref, vn_ref, ck_ref, cv_ref,
                     cq_ref, sq_ref, cp_ref, sp_ref, cn_ref, sn_ref, o_ref, *, layer, n_blocks):
    n = pl.program_id(1)
    past = ck_ref.shape[2]
    nk = 3 * BLOCK + past
    lane = lax.broadcasted_iota(jnp.int32, (BLOCK, HEAD_DIM), 1)
    lo_half = (lane % AXIS_DIM) < AXIS_PAIRS
    qi = lax.broadcasted_iota(jnp.int32, (BLOCK, nk), 0)
    kj = lax.broadcasted_iota(jnp.int32, (BLOCK, nk), 1)
    first_ok = jnp.where(n > 0, qi, BLOCK)
    last_ok = jnp.where(n < n_blocks - 1, qi, -1)
    bias_prev = jnp.where(kj >= first_ok, 0.0, NEG_INF)
    bias_next = jnp.where(kj - 2 * BLOCK <= last_ok, 0.0, NEG_INF)
    bias = jnp.where(kj < BLOCK, bias_prev, jnp.where(kj < 2 * BLOCK, 0.0, jnp.where(kj < 3 * BLOCK, bias_next, 0.0)))
    cq, sq = cq_ref[...], sq_ref[...]
    cq_scaled, sq_scaled = cq * (SCALE * LOG2E), sq * (SCALE * LOG2E)
    for hk in range(N_KV_HEADS):
        cols = slice(hk * HEAD_DIM, (hk + 1) * HEAD_DIM)
        heads = range(hk * GQA_GROUP, (hk + 1) * GQA_GROUP)
        keys = jnp.concatenate([
            _rope(kp_ref[:, cols], cp_ref[...], sp_ref[...], lo_half).astype(BF16),
            _rope(kc_ref[:, cols], cq, sq, lo_half).astype(BF16),
            _rope(kn_ref[:, cols], cn_ref[...], sn_ref[...], lo_half).astype(BF16),
            ck_ref[0, 0, :, cols].astype(BF16)], axis=0)
        vals = jnp.concatenate([vp_ref[:, cols].astype(BF16), vc_ref[:, cols].astype(BF16),
                                vn_ref[:, cols].astype(BF16), cv_ref[0, 0, :, cols].astype(BF16)], axis=0)
        q_heads = [_rope(q_ref[:, h * HEAD_DIM:(h + 1) * HEAD_DIM], cq_scaled, sq_scaled, lo_half) for h in heads]
        outs = _group_attention(q_heads, keys, vals, [sink_ref[layer, h] for h in heads], bias)
        for h, o in zip(heads, outs):
            o_ref[:, h * HEAD_DIM:(h + 1) * HEAD_DIM] = o.astype(o_ref.dtype)


def _lat_attention(proj, cache_k, cache_v, sink, rope_cos, rope_sin, layer, row0, n_seq, seq, into):
    nb = seq // BLOCK
    kb = ATTN_WIDTH // KV_WIDTH
    b0 = row0 // BLOCK
    past = cache_k.shape[2]

    def blk(shift):
        return lambda b, n: (b0 + b * nb + jnp.clip(n + shift, 0, nb - 1))

    def rows(shift, col):
        f = blk(shift)
        return lambda b, n: (f(b, n), col)

    def tab(shift):
        return pl.BlockSpec((BLOCK, HEAD_DIM), lambda b, n: (jnp.clip(n + shift, 0, nb - 1), 0))

    kv = lambda shift, col: pl.BlockSpec((BLOCK, KV_WIDTH), rows(shift, col))
    cache = pl.BlockSpec((1, 1, past, KV_WIDTH), lambda b, n: (b, layer, 0, 0))
    args = [sink, proj, proj, proj, proj, proj, proj, proj, cache_k, cache_v,
            rope_cos, rope_sin, rope_cos, rope_sin, rope_cos, rope_sin]
    specs = [pl.BlockSpec(memory_space=pltpu.SMEM),
             pl.BlockSpec((BLOCK, ATTN_WIDTH), rows(0, 0)),
             kv(-1, kb), kv(0, kb), kv(1, kb),
             kv(-1, kb + 1), kv(0, kb + 1), kv(1, kb + 1),
             cache, cache,
             tab(0), tab(0), tab(-1), tab(-1), tab(1), tab(1)]
    body, aliases = _reuse(functools.partial(_lat_attn_kernel, layer=layer, n_blocks=nb), args, specs, [into])
    return pl.pallas_call(
        body,
        grid=(n_seq, nb),
        in_specs=specs,
        out_specs=pl.BlockSpec((BLOCK, ATTN_WIDTH), lambda b, n: (b0 + b * nb + n, 0)),
        out_shape=jax.ShapeDtypeStruct(into.shape, into.dtype),
        input_output_aliases=aliases,
        compiler_params=_cparams(("arbitrary", "arbitrary"), 32 * MIB),
        name="latent_attention",
    )(*args)


def _conv3(u, w):
    n_rows = u.shape[0]
    row = lax.broadcasted_iota(jnp.int32, u.shape, 0)
    before = jnp.where(row == 0, 0.0, pltpu.roll(u, 1, 0))
    after = jnp.where(row == n_rows - 1, 0.0, pltpu.roll(u, n_rows - 1, 0))
    return before * w[0:1] + u * w[1:2] + after * w[2:3]


def _short_conv_kernel(b_ref, c_ref, x_ref, w_ref, o_ref):
    o_ref[...] = (b_ref[...] * _conv3(c_ref[...] * x_ref[...], w_ref[0])).astype(o_ref.dtype)


def _short_conv(proj, conv_w, layer, col0, row0, n_seq, seq, tc, out_rows, into=None):
    cw = conv_w.shape[2]
    tc = _tile(cw, tc)
    nc = cw // tc
    cb, rb = col0 // tc, row0 // seq

    def part(p):
        return pl.BlockSpec((seq, tc), lambda s, c: (rb + s, cb + p * nc + c))

    args = [proj, proj, proj, conv_w]
    specs = [part(0), part(1), part(2), pl.BlockSpec((1, 3, tc), lambda s, c: (layer, 0, c))]
    body, aliases = _reuse(_short_conv_kernel, args, specs, [into] if into is not None else None)
    return pl.pallas_call(
        body,
        grid=(n_seq, nc),
        in_specs=specs,
        out_specs=pl.BlockSpec((seq, tc), lambda s, c: (rb + s, c)),
        out_shape=jax.ShapeDtypeStruct((out_rows, cw), BF16),
        input_output_aliases=aliases,
        compiler_params=_cparams(("arbitrary", "arbitrary"), 2 * seq * tc * 14 + 6 * seq * tc * 4 + 8 * MIB),
        name="short_conv",
    )(*args)


def _hyena_kernel(x0_ref, x1_ref, v_ref, w0_ref, w1_ref, wv_ref, skip_ref, kre_ref, kim_ref,
                  c_ref, s_ref, ct_ref, st_ref, o_ref, g32_ref, g16_ref, acc_ref, *, n_k, scale):
    kk = pl.program_id(2)

    @pl.when(kk == 0)
    def _():
        g = _conv3(x1_ref[...], w1_ref[0]) * _conv3(v_ref[...], wv_ref[0])
        g32_ref[...] = g
        g16_ref[...] = g.astype(BF16)
        acc_ref[...] = jnp.zeros_like(acc_ref)

    g16 = g16_ref[...]
    u_c = jnp.dot(c_ref[...], g16, preferred_element_type=F32)
    u_s = jnp.dot(s_ref[...], g16, preferred_element_type=F32)
    kre, kim = kre_ref[0], kim_ref[0]
    y_re = u_c * kre + u_s * kim
    y_im = u_c * kim - u_s * kre
    acc_ref[...] += (jnp.dot(ct_ref[...], y_re.astype(BF16), preferred_element_type=F32)
                     - jnp.dot(st_ref[...], y_im.astype(BF16), preferred_element_type=F32))

    @pl.when(kk == n_k - 1)
    def _():
        y = acc_ref[...] * scale + skip_ref[0] * g32_ref[...]
        o_ref[...] = (_conv3(x0_ref[...], w0_ref[0]) * y).astype(o_ref.dtype)


def _hyena(proj, hyena_conv_w, hyena_skip, k_re, k_im, tabs, layer, col0, row0, n_seq, seq, tc, kc, out_rows,
           into=None):
    hw = hyena_skip.shape[2]
    tc, kc = _tile(hw, tc), _tile(seq, kc)
    nc, n_k = hw // tc, seq // kc
    cb, rb = col0 // tc, row0 // seq
    cos_b, sin_b, cos_t, sin_t = tabs

    def part(p):
        return pl.BlockSpec((seq, tc), lambda s, c, k: (rb + s, cb + p * nc + c))

    def wpart(p):
        return pl.BlockSpec((1, 3, tc), lambda s, c, k: (layer, 0, p * nc + c))

    spec_k = pl.BlockSpec((1, kc, tc), lambda s, c, k: (layer, k, c))
    fwd = pl.BlockSpec((kc, seq), lambda s, c, k: (k, 0))
    inv = pl.BlockSpec((seq, kc), lambda s, c, k: (0, k))
    args = [proj, proj, proj, hyena_conv_w, hyena_conv_w, hyena_conv_w, hyena_skip, k_re, k_im,
            cos_b, sin_b, cos_t, sin_t]
    specs = [part(0), part(1), part(2), wpart(0), wpart(1), wpart(2),
             pl.BlockSpec((1, 1, tc), lambda s, c, k: (layer, 0, c)),
             spec_k, spec_k, fwd, fwd, inv, inv]
    body, aliases = _reuse(functools.partial(_hyena_kernel, n_k=n_k, scale=1.0 / seq), args, specs,
                           [into] if into is not None else None)
    return pl.pallas_call(
        body,
        grid=(n_seq, nc, n_k),
        in_specs=specs,
        out_specs=pl.BlockSpec((seq, tc), lambda s, c, k: (rb + s, c)),
        out_shape=jax.ShapeDtypeStruct((out_rows, hw), BF16),
        input_output_aliases=aliases,
        scratch_shapes=[pltpu.VMEM((seq, tc), F32), pltpu.VMEM((seq, tc), BF16), pltpu.VMEM((seq, tc), F32)],
        compiler_params=_cparams(("arbitrary", "arbitrary", "arbitrary"),
                                 2 * (3 * seq * tc * 4 + seq * tc * 2 + 2 * kc * tc * 4 + 4 * kc * seq * 2)
                                 + seq * tc * 10 + 6 * seq * tc * 4 + 8 * MIB),
        name="hyena",
    )(*args)


def _split_bf16(x):
    hi = x.astype(BF16)
    return hi, (x - hi.astype(F32)).astype(BF16)


def _dot_split(a_hi, a_lo, b_hi, b_lo):
    return (jnp.dot(a_hi, b_hi, preferred_element_type=F32) + jnp.dot(a_hi, b_lo, preferred_element_type=F32)
            + jnp.dot(a_lo, b_hi, preferred_element_type=F32))


def _filter_kernel(z_ref, w1_ref, b1_ref, fr_ref, w2_ref, b2_ref, w3f_ref, w3b_ref, decay_ref,
                   ch_ref, cl_ref, sh_ref, sl_ref, kre_ref, kim_ref, hph_ref, hpl_ref, hmh_ref, hml_ref):
    @pl.when(pl.program_id(2) == 0)
    def _():
        fr = fr_ref[0]
        hdn = jnp.sin(fr * (jnp.dot(z_ref[...], w1_ref[0], precision=HI, preferred_element_type=F32) + b1_ref[0]))
        hdn = jnp.sin(fr * (jnp.dot(hdn, w2_ref[0], precision=HI, preferred_element_type=F32) + b2_ref[0]))
        decay = decay_ref[...]
        h_fwd = jnp.dot(hdn, w3f_ref[0], precision=HI, preferred_element_type=F32) * decay
        h_bwd = jnp.dot(hdn, w3b_ref[0], precision=HI, preferred_element_type=F32) * decay
        row = lax.broadcasted_iota(jnp.int32, h_bwd.shape, 0)
        h_bwd = jnp.where(row == 0, 0.0, h_bwd)
        hph_ref[...], hpl_ref[...] = _split_bf16(h_fwd + h_bwd)
        hmh_ref[...], hml_ref[...] = _split_bf16(h_fwd - h_bwd)

    kre_ref[0] = _dot_split(ch_ref[...], cl_ref[...], hph_ref[...], hpl_ref[...])
    kim_ref[0] = -_dot_split(sh_ref[...], sl_ref[...], hmh_ref[...], hml_ref[...])


def _filter_spectra(z, w1, b1, fr, w2, b2, w3, decay, tabs, tc, kc):
    depth = w1.shape[0]
    seq, hw = decay.shape
    tc, kc = _tile(hw, tc), _tile(seq, kc)
    nc, n_k = hw // tc, seq // kc
    small = lambda shape: pl.BlockSpec((1,) + shape, lambda l, c, k: (l, 0, 0))
    out = pl.BlockSpec((1, kc, tc), lambda l, c, k: (l, k, c))
    tab = pl.BlockSpec((kc, seq), lambda l, c, k: (k, 0))
    return pl.pallas_call(
        _filter_kernel,
        grid=(depth, nc, n_k),
        in_specs=[pl.BlockSpec((seq, FILT_PAD), lambda l, c, k: (0, 0)),
                  small((FILT_PAD, FILT_PAD)), small((1, FILT_PAD)), small((1, FILT_PAD)),
                  small((FILT_PAD, FILT_PAD)), small((1, FILT_PAD)),
                  pl.BlockSpec((1, FILT_PAD, tc), lambda l, c, k: (l, 0, c)),
                  pl.BlockSpec((1, FILT_PAD, tc), lambda l, c, k: (l, 0, nc + c)),
                  pl.BlockSpec((seq, tc), lambda l, c, k: (0, c)),
                  tab, tab, tab, tab],
        out_specs=[out, out],
        out_shape=[jax.ShapeDtypeStruct((depth, seq, hw), F32)] * 2,
        scratch_shapes=[pltpu.VMEM((seq, tc), BF16)] * 4,
        compiler_params=_cparams(("arbitrary", "arbitrary", "arbitrary"),
                                 2 * (seq * tc * 4 + 4 * kc * seq * 2 + 2 * kc * tc * 4 + seq * FILT_PAD * 4)
                                 + 6 * seq * tc * 4 + 8 * MIB),
        name="filter_spectra",
    )(z, w1, b1, fr, w2, b2, w3, w3, decay, *tabs)


def _rope_tables(seq):
    rows = seq // GRID_W
    t_row = jnp.repeat(jnp.arange(rows, dtype=F32), GRID_W)
    t_col = jnp.tile(jnp.arange(GRID_W, dtype=F32), rows)
    inv = ROPE_BASE ** (-jnp.arange(0, AXIS_DIM, 2, dtype=F32) / AXIS_DIM)
    c_r, s_r = jnp.cos(t_row[:, None] * inv), jnp.sin(t_row[:, None] * inv)
    c_c, s_c = jnp.cos(t_col[:, None] * inv), jnp.sin(t_col[:, None] * inv)
    cos = jnp.concatenate([c_r, c_r, c_c, c_c], axis=-1)
    sin_signed = jnp.concatenate([-s_r, s_r, -s_c, s_c], axis=-1)
    return cos, sin_signed


def _filter_features(seq):
    pos = jnp.arange(seq, dtype=F32)
    t = jnp.linspace(0.0, 1.0, seq, dtype=F32)
    bands = (FILTER_EMB - 1) // 2
    f = jnp.linspace(1e-4, bands - 1, bands, dtype=F32)
    ang = 2.0 * math.pi * pos[:, None] * f[None, :] / seq
    z = jnp.concatenate([t[:, None], jnp.cos(ang), -jnp.sin(ang)], axis=-1)
    return jnp.pad(z, ((0, 0), (0, FILT_PAD - FILTER_EMB)))


def _filter_decay(seq, width):
    t = jnp.linspace(0.0, 1.0, seq, dtype=F32)
    deltas = jnp.abs(jnp.linspace(math.log(HYENA_TARGET) / SLOW_DECAY_PCT,
                                  math.log(HYENA_TARGET) / FAST_DECAY_PCT, width, dtype=F32))
    return jnp.exp(-t[:, None] * deltas[None, :])


def _odd_dft_tables(seq):
    k = jnp.arange(seq, dtype=jnp.int32)[:, None]
    t = jnp.arange(seq, dtype=jnp.int32)[None, :]
    phase = ((2 * k + 1) * t) % (4 * seq)
    ang = phase.astype(F32) * (math.pi / (2 * seq))
    return jnp.cos(ang), jnp.sin(ang)


def kernel(x_prompt, x_sample, cache_k, cache_v, c, c_ctx, norm_g, w_mod, b_mod, w_in, attn_sink, conv_w, hyena_conv_w,
           filt_w1, filt_b1, filt_freq, filt_w2, filt_b2, filt_w3, hyena_skip, w_branch, w_gate, b_gate, w_o, w_ffn_in,
           w_ffn_out):
    n_ctx, s_ctx, d = x_prompt.shape
    n_lat, s_lat, _ = x_sample.shape
    depth = w_in.shape[0]
    cw = conv_w.shape[2]
    hw = hyena_skip.shape[1]
    d_ff = w_ffn_out.shape[1]
    past = cache_k.shape[2]
    m_ctx, m_lat = n_ctx * s_ctx, n_lat * s_lat
    m = m_ctx + m_lat
    tm_row = s_ctx
    assert s_lat % tm_row == 0 and m_ctx % s_lat == 0 and n_lat + 1 <= MOD_ROWS
    col_conv = ATTN_WIDTH + 2 * KV_WIDTH
    col_hy = col_conv + 3 * cw

    def seq_of_tile(i):
        n_ctx_tiles, per_lat = m_ctx // tm_row, s_lat // tm_row
        return jnp.where(i < n_ctx_tiles, 0, 1 + (i - n_ctx_tiles) // per_lat)

    w_in_b = w_in[:1].astype(BF16)
    w_gate_b = w_gate[:1].astype(BF16)

    cvecs = jnp.concatenate([c_ctx[None], c, jnp.zeros((MOD_ROWS - 1 - n_lat, d), F32)], axis=0)
    mod_tab = _modulations(cvecs, w_mod, b_mod).reshape(depth * MOD_ROWS * N_MOD, 1, d)
    norm_tab = norm_g.reshape(depth * 4, 1, d)

    fp = FILT_PAD - filt_w1.shape[2]
    f_w1 = jnp.pad(filt_w1, ((0, 0), (0, FILT_PAD - FILTER_EMB), (0, fp)))
    f_b1 = jnp.pad(filt_b1, ((0, 0), (0, fp)))[:, None]
    f_fr = jnp.pad(filt_freq, ((0, 0), (0, fp)))[:, None]
    f_w2 = jnp.pad(filt_w2, ((0, 0), (0, fp), (0, fp)))
    f_b2 = jnp.pad(filt_b2, ((0, 0), (0, fp)))[:, None]
    f_w3 = jnp.pad(filt_w3, ((0, 0), (0, fp), (0, 0)))
    spectra, dft = {}, {}
    for seq in sorted({s_ctx, s_lat}):
        cos32, sin32 = _odd_dft_tables(seq)
        cos_b, sin_b = cos32.astype(BF16), sin32.astype(BF16)
        cos_lo, sin_lo = (cos32 - cos_b.astype(F32)).astype(BF16), (sin32 - sin_b.astype(F32)).astype(BF16)
        spectra[seq] = _filter_spectra(_filter_features(seq), f_w1, f_b1, f_fr, f_w2, f_b2, f_w3,
                                       _filter_decay(seq, hw), (cos_b, cos_lo, sin_b, sin_lo), 512, 256)
        dft[seq] = (cos_b, sin_b, cos_b.T, sin_b.T)

    rope_cos, rope_sin = _rope_tables(s_lat)
    cache_k2 = cache_k.reshape(n_lat, depth, past, KV_WIDTH)
    cache_v2 = cache_v.reshape(n_lat, depth, past, KV_WIDTH)
    b_gate3 = b_gate.reshape(depth, 1, -1)
    skip3 = hyena_skip.reshape(depth, 1, hw)

    groups = ((0, m_ctx), (m_ctx, m_lat))

    def res_stage(x_parts, y, split_out, **kw):
        if len(x_parts) == 1 and not split_out:
            return _resnorm(x_parts[0][0], y, norm_tab, mod_tab, seq_of_tile, tm_row, rows=(0, m), **kw)
        outs, per_group = None, []
        for gi, (g0, n_rows) in enumerate(groups):
            xa, xb = x_parts[gi] if len(x_parts) > 1 else x_parts[0]
            if split_out:
                per_group.append(_resnorm(xa, y, norm_tab, mod_tab, seq_of_tile, tm_row, rows=(g0, n_rows), x_base=xb,
                                          out_base=g0, **kw))
            else:
                outs = _resnorm(xa, y, norm_tab, mod_tab, seq_of_tile, tm_row, rows=(g0, n_rows), x_base=xb,
                                out_rows=m, into=outs, **kw)
        return per_group if split_out else outs

    x_parts = [(x_prompt.reshape(m_ctx, d), 0), (x_sample.reshape(m_lat, d), m_ctx)]
    (h,) = res_stage(x_parts, None, False, h_norm=0, h_scale=(0, 1), h_shift=(0, 0))
    new_k, new_v = [], []
    for l in range(depth):
        proj, (w_branch_b,) = _matmul(h, w_in_b, 0, F32, 1024, 1024, [(w_branch, l)])
        gates, (w_o_b,) = _gates(h, w_gate_b, b_gate3, 0, l, 1024, 1024, [(w_o, l)])
        new_k.append(proj[:m_ctx, ATTN_WIDTH:ATTN_WIDTH + KV_WIDTH].reshape(n_ctx, s_ctx, N_KV_HEADS, HEAD_DIM))
        new_v.append(proj[:m_ctx, ATTN_WIDTH + KV_WIDTH:col_conv].reshape(n_ctx, s_ctx, N_KV_HEADS, HEAD_DIM))

        o_attn = _ctx_attention(proj, attn_sink, l, n_ctx, s_ctx, m)
        o_attn = _lat_attention(proj, cache_k2, cache_v2, attn_sink, rope_cos, rope_sin, l, m_ctx, n_lat, s_lat, o_attn)
        o_conv = _short_conv(proj, conv_w, l, col_conv, 0, n_ctx, s_ctx, 512, m)
        o_conv = _short_conv(proj, conv_w, l, col_conv, m_ctx, n_lat, s_lat, 256, m, o_conv)
        o_hy = _hyena(proj, hyena_conv_w, skip3, *spectra[s_ctx], dft[s_ctx], l, col_hy, 0, n_ctx, s_ctx, 512, 512, m)
        o_hy = _hyena(proj, hyena_conv_w, skip3, *spectra[s_lat], dft[s_lat], l, col_hy, m_ctx, n_lat, s_lat, 512, 256,
                      m, o_hy)

        merged, (w_ffn_in_b,) = _merge(o_attn, o_conv, o_hy, w_branch_b[None], gates, 0, 1024, 512, [(w_ffn_in, l)])
        mix, _ = _matmul(merged, w_o_b[None], 0, BF16, 1024, 1024)
        x, h = res_stage(x_parts, mix, False, y_norm=l * 4 + 1, gate=(l, 2),
                         h_norm=l * 4 + 2, h_scale=(l, 4), h_shift=(l, 3))
        x_parts = [(x, 0)]
        next_casts = [(w_in, l + 1), (w_gate, l + 1)] if l + 1 < depth else []
        hidden, (w_ffn_out_b, *next_w) = _ffn_in(h, w_ffn_in_b[None], 0, 1024, 2, [(w_ffn_out, l)] + next_casts)
        if next_w:
            w_in_b, w_gate_b = next_w[0][None], next_w[1][None]
        f, _ = _matmul(hidden, w_ffn_out_b[None], 0, BF16, 512, 512)
        if l + 1 < depth:
            x, h = res_stage(x_parts, f, False, y_norm=l * 4 + 3, gate=(l, 5),
                             h_norm=(l + 1) * 4, h_scale=(l + 1, 1), h_shift=(l + 1, 0))
            x_parts = [(x, 0)]
        else:
            (y_prompt,), (y_sample,) = res_stage(x_parts, f, True, y_norm=l * 4 + 3, gate=(l, 5))

    return (y_prompt.reshape(n_ctx, s_ctx, d), y_sample.reshape(n_lat, s_lat, d),
            jnp.stack(new_k, axis=1), jnp.stack(new_v, axis=1))
```

```python
import functools
import math

import jax
import jax.numpy as jnp
from jax import lax
from jax.experimental import pallas as pl
from jax.experimental.pallas import tpu as pltpu

F32 = jnp.float32
BF16 = jnp.bfloat16

HEAD_DIM = 128
N_HEADS = 16
N_KV_HEADS = 4
GQA_GROUP = N_HEADS // N_KV_HEADS
ATTN_WIDTH = N_HEADS * HEAD_DIM
KV_WIDTH = N_KV_HEADS * HEAD_DIM
WINDOW = 128
BLOCK = 128
GRID_W = 64
ROPE_BASE = 10000.0
AXIS_DIM = HEAD_DIM // 2
AXIS_PAIRS = AXIS_DIM // 2
FILTER_EMB = 33
HYENA_TARGET = 1e-2
FAST_DECAY_PCT = 0.3
SLOW_DECAY_PCT = 1.5
EPS = 1e-6
SCALE = HEAD_DIM ** -0.5
LOG2E = math.log2(math.e)
NEG_INF = -1e30
N_MOD = 6
MOD_ROWS = 8
FILT_PAD = 128
DFT_SPLIT = 64

LANE = 128
MXU_COLS = 256
BF16_ROWS = 16
MIB = 1024 * 1024
VMEM_CAP_BYTES = 56 * MIB
HI = lax.Precision.HIGHEST


def _cparams(semantics, vmem_bytes):
    return pltpu.CompilerParams(dimension_semantics=semantics,
                                vmem_limit_bytes=int(min(max(vmem_bytes, 16 * MIB), VMEM_CAP_BYTES)))


def _tile(n, pref):
    if n <= pref:
        return n
    t = (pref // LANE) * LANE
    while t > LANE and n % t:
        t -= LANE
    assert n % t == 0, (n, pref)
    return t


def _skip_alias(body, n_in, n_alias, *refs):
    body(*refs[:n_in], *refs[n_in + n_alias:])


def _reuse(body, args, specs, into):
    if not into:
        return body, {}
    n_in = len(args)
    aliases = {n_in + k: k for k in range(len(into))}
    args += list(into)
    specs += [pl.BlockSpec(memory_space=pl.ANY)] * len(into)
    return functools.partial(_skip_alias, body, n_in, len(into)), aliases


def _mods_kernel(c_ref, w_ref, b_ref, o_ref):
    c = c_ref[...]
    a = (c * jax.nn.sigmoid(c)).astype(BF16)
    o_ref[0] = jnp.dot(a, w_ref[0].astype(BF16), preferred_element_type=F32) + b_ref[0]


def _modulations(cvecs, w_mod, b_mod):
    depth, d, n = w_mod.shape
    tn = _tile(n, 512)
    return pl.pallas_call(
        _mods_kernel,
        grid=(depth, n // tn),
        in_specs=[
            pl.BlockSpec((MOD_ROWS, d), lambda l, j: (0, 0)),
            pl.BlockSpec((1, d, tn), lambda l, j: (l, 0, j)),
            pl.BlockSpec((1, 1, tn), lambda l, j: (l, 0, j)),
        ],
        out_specs=pl.BlockSpec((1, MOD_ROWS, tn), lambda l, j: (l, 0, j)),
        out_shape=jax.ShapeDtypeStruct((depth, MOD_ROWS, n), F32),
        compiler_params=_cparams(("arbitrary", "arbitrary"), 2 * d * tn * 4 + 8 * MIB),
        name="modulations",
    )(cvecs, w_mod, b_mod.reshape(depth, 1, n))


def _rms(x):
    return x * lax.rsqrt(jnp.mean(x * x, axis=-1, keepdims=True) + EPS)


def _resnorm_kernel(*refs, has_y, has_h):
    it = iter(refs)
    x_ref = next(it)
    if has_y:
        y_ref, gny_ref, gate_ref = next(it), next(it), next(it)
    if has_h:
        gnx_ref, sc_ref, sh_ref = next(it), next(it), next(it)
    if has_y:
        xo_ref = next(it)
    if has_h:
        h_ref = next(it)
    x = x_ref[...]
    if has_y:
        x = x + gate_ref[0] * (_rms(y_ref[...].astype(F32)) * gny_ref[0])
        xo_ref[...] = x
    if has_h:
        h_ref[...] = ((_rms(x) * gnx_ref[0]) * (1.0 + sc_ref[0]) + sh_ref[0]).astype(h_ref.dtype)


def _resnorm(x, y, norm_tab, mod_tab, seq_of_tile, tm, *, rows, x_base=0, out_base=0, out_rows=None, into=None,
             y_norm=None, gate=None, h_norm=None, h_scale=None, h_shift=None):
    g0, n_rows = rows
    d = x.shape[1]
    out_rows = n_rows if out_rows is None else out_rows
    has_y, has_h = y is not None, h_norm is not None
    assert g0 % tm == 0 and n_rows % tm == 0 and x_base % tm == 0 and out_base % tm == 0
    xt, gt, ot = (g0 - x_base) // tm, g0 // tm, (g0 - out_base) // tm
    x_row = pl.BlockSpec((tm, d), lambda i: (i + xt, 0))
    y_row = pl.BlockSpec((tm, d), lambda i: (i + gt, 0))
    o_row = pl.BlockSpec((tm, d), lambda i: (i + ot, 0))

    def tab(idx):
        return pl.BlockSpec((1, 1, d), lambda i: (idx, 0, 0))

    def mod(layer_which):
        layer, which = layer_which
        return pl.BlockSpec((1, 1, d), lambda i: ((layer * MOD_ROWS + seq_of_tile(i + gt)) * N_MOD + which, 0, 0))

    args, specs = [x], [x_row]
    if has_y:
        args += [y, norm_tab, mod_tab]
        specs += [y_row, tab(y_norm), mod(gate)]
    if has_h:
        args += [norm_tab, mod_tab, mod_tab]
        specs += [tab(h_norm), mod(h_scale), mod(h_shift)]
    out_shape, out_specs = [], []
    if has_y:
        out_shape.append(jax.ShapeDtypeStruct((out_rows, d), F32))
        out_specs.append(o_row)
    if has_h:
        out_shape.append(jax.ShapeDtypeStruct((out_rows, d), BF16))
        out_specs.append(o_row)
    body, aliases = _reuse(functools.partial(_resnorm_kernel, has_y=has_y, has_h=has_h), args, specs, into)
    return pl.pallas_call(
        body,
        grid=(n_rows // tm,),
        in_specs=specs,
        out_specs=out_specs,
        out_shape=out_shape,
        input_output_aliases=aliases,
        compiler_params=_cparams(("arbitrary",), 2 * tm * d * 14 + 8 * MIB),
        name="resnorm",
    )(*args)


def _with_casts(body, n_in, n_cast, n_out, *refs):
    srcs = refs[n_in:n_in + n_cast]
    dsts = refs[n_in + n_cast + n_out:n_in + 2 * n_cast + n_out]
    body(*refs[:n_in], *refs[n_in + n_cast:n_in + n_cast + n_out], *refs[n_in + 2 * n_cast + n_out:])
    for src, dst in zip(srcs, dsts):
        dst[...] = src[...].astype(dst.dtype)


def _dense_call(body, args, specs, grid, out_shape, out_spec, vmem_bytes, name, casts=(), into=None):
    nj = grid[1]
    steps = grid[0] * nj
    n_in = len(args)
    args, specs = list(args), list(specs)
    single = not isinstance(out_shape, (list, tuple))
    out_shapes, out_specs = ([out_shape], [out_spec]) if single else (list(out_shape), list(out_spec))
    n_out = len(out_shapes)
    for w, layer in casts:
        k, cols = w.shape[1], w.shape[2]
        rows = BF16_ROWS
        while k % rows or k // rows > steps:
            rows += BF16_ROWS
        n_blocks = k // rows
        blk = lambda i, j, n_blocks=n_blocks: jnp.minimum(i * nj + j, n_blocks - 1)
        args.append(w.reshape(-1, cols))
        specs.append(pl.BlockSpec((rows, cols), lambda i, j, blk=blk, first=layer * n_blocks: (first + blk(i, j), 0)))
        out_shapes.append(jax.ShapeDtypeStruct((k, cols), BF16))
        out_specs.append(pl.BlockSpec((rows, cols), lambda i, j, blk=blk: (blk(i, j), 0)))
        vmem_bytes += 2 * rows * cols * 6
    kernel = functools.partial(_with_casts, body, n_in, len(casts), n_out)
    kernel, aliases = _reuse(kernel, args, specs, into)
    outs = pl.pallas_call(
        kernel, grid=grid, in_specs=specs, out_specs=out_specs, out_shape=out_shapes, input_output_aliases=aliases,
        compiler_params=_cparams(("arbitrary", "arbitrary"), vmem_bytes), name=name,
    )(*args)
    return (outs[0] if single else list(outs[:n_out])), list(outs[n_out:])


def _mm_kernel(a_ref, w_ref, o_ref):
    o_ref[...] = jnp.dot(a_ref[...], w_ref[0], preferred_element_type=F32).astype(o_ref.dtype)


def _matmul(a, w, layer, out_dtype, tm, tn, casts=()):
    m, k = a.shape
    n = w.shape[2]
    tm, tn = _tile(m, tm), _tile(n, tn)
    ob = jnp.dtype(out_dtype).itemsize
    return _dense_call(
        _mm_kernel, [a, w],
        [pl.BlockSpec((tm, k), lambda i, j: (i, 0)), pl.BlockSpec((1, k, tn), lambda i, j: (layer, 0, j))],
        (m // tm, n // tn), jax.ShapeDtypeStruct((m, n), out_dtype), pl.BlockSpec((tm, tn), lambda i, j: (i, j)),
        2 * (tm * k * 2 + k * tn * 2 + tm * tn * ob) + 2 * tm * tn * 4 + 8 * MIB, "matmul", casts)


def _sigmoid(x):
    return 0.5 * jnp.tanh(0.5 * x) + 0.5


def _gate_kernel(a_ref, w_ref, b_ref, o_ref):
    acc = jnp.dot(a_ref[...], w_ref[0], preferred_element_type=F32) + b_ref[0]
    o_ref[...] = _sigmoid(acc).astype(o_ref.dtype)


def _gates(h, w_gate, b_gate, layer, b_layer, tm, tn, casts=()):
    m, k = h.shape
    n = w_gate.shape[2]
    tm, tn = _tile(m, tm), _tile(n, tn)
    return _dense_call(
        _gate_kernel, [h, w_gate, b_gate],
        [pl.BlockSpec((tm, k), lambda i, j: (i, 0)),
         pl.BlockSpec((1, k, tn), lambda i, j: (layer, 0, j)),
         pl.BlockSpec((1, 1, tn), lambda i, j: (b_layer, 0, j))],
        (m // tm, n // tn), jax.ShapeDtypeStruct((m, n), BF16), pl.BlockSpec((tm, tn), lambda i, j: (i, j)),
        2 * (tm * k * 2 + k * tn * 2 + tm * tn * 2) + 2 * tm * tn * 4 + 8 * MIB, "gates", casts)


def _merge_kernel(oa_ref, oc_ref, oh_ref, wa_ref, wc_ref, wh_ref, ga_ref, gc_ref, gh_ref, o_ref):
    ba = jnp.dot(oa_ref[...], wa_ref[0], preferred_element_type=F32)
    bc = jnp.dot(oc_ref[...], wc_ref[0], preferred_element_type=F32)
    bh = jnp.dot(oh_ref[...], wh_ref[0], preferred_element_type=F32)
    merged = ga_ref[...].astype(F32) * ba + gc_ref[...].astype(F32) * bc + gh_ref[...].astype(F32) * bh
    o_ref[...] = merged.astype(o_ref.dtype)


def _merge(o_attn, o_conv, o_hy, w_branch, gates, layer, tm, tn, casts=()):
    m = o_attn.shape[0]
    d = w_branch.shape[2]
    cw = o_conv.shape[1]
    assert ATTN_WIDTH % cw == 0
    tm, tn = _tile(m, tm), _tile(d, tn)
    nj = d // tn
    conv_blk = ATTN_WIDTH // cw
    return _dense_call(
        _merge_kernel, [o_attn, o_conv, o_hy, w_branch, w_branch, w_branch, gates, gates, gates],
        [pl.BlockSpec((tm, ATTN_WIDTH), lambda i, j: (i, 0)),
         pl.BlockSpec((tm, cw), lambda i, j: (i, 0)),
         pl.BlockSpec((tm, cw), lambda i, j: (i, 0)),
         pl.BlockSpec((1, ATTN_WIDTH, tn), lambda i, j: (layer, 0, j)),
         pl.BlockSpec((1, cw, tn), lambda i, j: (layer, conv_blk, j)),
         pl.BlockSpec((1, cw, tn), lambda i, j: (layer, conv_blk + 1, j)),
         pl.BlockSpec((tm, tn), lambda i, j: (i, j)),
         pl.BlockSpec((tm, tn), lambda i, j: (i, nj + j)),
         pl.BlockSpec((tm, tn), lambda i, j: (i, 2 * nj + j))],
        (m // tm, nj), jax.ShapeDtypeStruct((m, d), BF16), pl.BlockSpec((tm, tn), lambda i, j: (i, j)),
        2 * (tm * (ATTN_WIDTH + 2 * cw) * 2 + (ATTN_WIDTH + 2 * cw) * tn * 2 + 4 * tm * tn * 2)
        + 3 * tm * tn * 4 + 8 * MIB, "merge", casts)


def _ffn_in_kernel(h_ref, *refs, n_sub):
    w_refs, o_ref = refs[:-1], refs[-1]
    h = h_ref[...]
    for s in range(n_sub):
        a = jnp.dot(h, w_refs[s][0], preferred_element_type=F32)
        b = jnp.dot(h, w_refs[n_sub + s][0], preferred_element_type=F32)
        o_ref[:, s * MXU_COLS:(s + 1) * MXU_COLS] = (a * _sigmoid(a) * b).astype(o_ref.dtype)


def _ffn_in(h, w_ffn_in, layer, tm, n_sub, casts=()):
    m, k = h.shape
    ff = w_ffn_in.shape[2] // 2
    assert ff % MXU_COLS == 0
    tm = _tile(m, tm)
    nb = ff // MXU_COLS

    def call(first, n_tiles, width, casts, into):
        assert first % width == 0
        tn = width * MXU_COLS
        w_spec = lambda half, s: pl.BlockSpec(
            (1, k, MXU_COLS), lambda i, j: (layer, 0, half * nb + first + width * j + s))
        return _dense_call(
            functools.partial(_ffn_in_kernel, n_sub=width), [h] + [w_ffn_in] * (2 * width),
            [pl.BlockSpec((tm, k), lambda i, j: (i, 0))]
            + [w_spec(0, s) for s in range(width)] + [w_spec(1, s) for s in range(width)],
            (m // tm, n_tiles), jax.ShapeDtypeStruct((m, ff), BF16),
            pl.BlockSpec((tm, tn), lambda i, j: (i, first // width + j)),
            2 * (tm * k * 2 + 2 * k * tn * 2 + tm * tn * 2) + 2 * tm * tn * 4 + 8 * MIB, "ffn_in", casts, into)

    hidden, cast_out = call(0, nb // n_sub, n_sub, casts, None)
    if nb % n_sub:
        hidden, _ = call(nb - nb % n_sub, 1, nb % n_sub, (), [hidden])
    return hidden, cast_out


def _group_attention(q_heads, keys, vals, sinks, bias):
    rows = q_heads[0].shape[0]
    q = jnp.concatenate([qh.astype(BF16) for qh in q_heads], axis=0)
    s = lax.dot_general(q, keys, (((1,), (1,)), ((), ())), preferred_element_type=F32)
    probs, denoms = [], []
    for g, sk in enumerate(sinks):
        sg = s[g * rows:(g + 1) * rows]
        if bias is not None:
            sg = sg + bias
        sink = sk * LOG2E
        m = jnp.maximum(jnp.max(sg, axis=-1, keepdims=True), sink)
        p = jnp.exp2(sg - m)
        denoms.append(jnp.sum(p, axis=-1, keepdims=True) + jnp.exp2(sink - m))
        probs.append(p.astype(BF16))
    o = jnp.dot(jnp.concatenate(probs, axis=0), vals, preferred_element_type=F32)
    return [o[g * rows:(g + 1) * rows] / denoms[g] for g in range(len(q_heads))]


def _ctx_attn_kernel(sink_ref, q_ref, k_ref, v_ref, new_k_ref, new_v_ref, o_ref, *, layer):
    new_k_ref[0, 0] = k_ref[...]
    new_v_ref[0, 0] = v_ref[...]
    for hk in range(N_KV_HEADS):
        cols = slice(hk * HEAD_DIM, (hk + 1) * HEAD_DIM)
        heads = range(hk * GQA_GROUP, (hk + 1) * GQA_GROUP)
        q_heads = [q_ref[:, h * HEAD_DIM:(h + 1) * HEAD_DIM] * (SCALE * LOG2E) for h in heads]
        outs = _group_attention(q_heads, k_ref[:, cols].astype(BF16), v_ref[:, cols].astype(BF16),
                                [sink_ref[layer, h] for h in heads], None)
        for h, o in zip(heads, outs):
            o_ref[:, h * HEAD_DIM:(h + 1) * HEAD_DIM] = o.astype(o_ref.dtype)


def _ctx_attention(proj, sink, layer, n_seq, seq, out_rows, depth, new_kv):
    kb = ATTN_WIDTH // KV_WIDTH
    kv_shape = jax.ShapeDtypeStruct((n_seq, depth, seq, KV_WIDTH), F32)
    kv_spec = pl.BlockSpec((1, 1, seq, KV_WIDTH), lambda b: (b, layer, 0, 0))
    args = [sink, proj, proj, proj]
    specs = [pl.BlockSpec(memory_space=pltpu.SMEM),
             pl.BlockSpec((seq, ATTN_WIDTH), lambda b: (b, 0)),
             pl.BlockSpec((seq, KV_WIDTH), lambda b: (b, kb)),
             pl.BlockSpec((seq, KV_WIDTH), lambda b: (b, kb + 1))]
    body, aliases = _reuse(functools.partial(_ctx_attn_kernel, layer=layer), args, specs, new_kv)
    new_k, new_v, o = pl.pallas_call(
        body,
        grid=(n_seq,),
        in_specs=specs,
        out_specs=[kv_spec, kv_spec, pl.BlockSpec((seq, ATTN_WIDTH), lambda b: (b, 0))],
        out_shape=[kv_shape, kv_shape, jax.ShapeDtypeStruct((out_rows, ATTN_WIDTH), BF16)],
        input_output_aliases=aliases,
        compiler_params=_cparams(("arbitrary",), 32 * MIB),
        name="ctx_attention",
    )(*args)
    return o, [new_k, new_v]


def _rope(x, cos, sin_signed, lo_half):
    partner = jnp.where(lo_half, pltpu.roll(x, HEAD_DIM - AXIS_PAIRS, 1), pltpu.roll(x, AXIS_PAIRS, 1))
    return x * cos + partner * sin_signed


def _lat_attn_kernel(sink_ref, q_ref, kp_ref, kc_ref, kn_ref, vp_ref, vc_ref, vn_ref, ck_ref, cv_ref,
                     cq_ref, sq_ref, cp_ref, sp_ref, cn_ref, sn_ref, o_ref, *, layer, n_blocks):
    n = pl.program_id(1)
    past = ck_ref.shape[2]
    nk = 3 * BLOCK + past
    lane = lax.broadcasted_iota(jnp.int32, (BLOCK, HEAD_DIM), 1)
    lo_half = (lane % AXIS_DIM) < AXIS_PAIRS
    qi = lax.broadcasted_iota(jnp.int32, (BLOCK, nk), 0)
    kj = lax.broadcasted_iota(jnp.int32, (BLOCK, nk), 1)
    first_ok = jnp.where(n > 0, qi, BLOCK)
    last_ok = jnp.where(n < n_blocks - 1, qi, -1)
    bias_prev = jnp.where(kj >= first_ok, 0.0, NEG_INF)
    bias_next = jnp.where(kj - 2 * BLOCK <= last_ok, 0.0, NEG_INF)
    bias = jnp.where(kj < BLOCK, bias_prev, jnp.where(kj < 2 * BLOCK, 0.0, jnp.where(kj < 3 * BLOCK, bias_next, 0.0)))
    cq, sq = cq_ref[...], sq_ref[...]
    cq_scaled, sq_scaled = cq * (SCALE * LOG2E), sq * (SCALE * LOG2E)
    for hk in range(N_KV_HEADS):
        cols = slice(hk * HEAD_DIM, (hk + 1) * HEAD_DIM)
        heads = range(hk * GQA_GROUP, (hk + 1) * GQA_GROUP)
        keys = jnp.concatenate([
            _rope(kp_ref[:, cols], cp_ref[...], sp_ref[...], lo_half).astype(BF16),
            _rope(kc_ref[:, cols], cq, sq, lo_half).astype(BF16),
            _rope(kn_ref[:, cols], cn_ref[...], sn_ref[...], lo_half).astype(BF16),
            ck_ref[0, 0, :, cols].astype(BF16)], axis=0)
        vals = jnp.concatenate([vp_ref[:, cols].astype(BF16), vc_ref[:, cols].astype(BF16),
                                vn_ref[:, cols].astype(BF16), cv_ref[0, 0, :, cols].astype(BF16)], axis=0)
        q_heads = [_rope(q_ref[:, h * HEAD_DIM:(h + 1) * HEAD_DIM], cq_scaled, sq_scaled, lo_half) for h in heads]
        outs = _group_attention(q_heads, keys, vals, [sink_ref[layer, h] for h in heads], bias)
        for h, o in zip(heads, outs):
            o_ref[:, h * HEAD_DIM:(h + 1) * HEAD_DIM] = o.astype(o_ref.dtype)


def _lat_attention(proj, cache_k, cache_v, sink, rope_cos, rope_sin, layer, row0, n_seq, seq, into):
    nb = seq // BLOCK
    kb = ATTN_WIDTH // KV_WIDTH
    b0 = row0 // BLOCK
    past = cache_k.shape[2]

    def blk(shift):
        return lambda b, n: (b0 + b * nb + jnp.clip(n + shift, 0, nb - 1))

    def rows(shift, col):
        f = blk(shift)
        return lambda b, n: (f(b, n), col)

    def tab(shift):
        return pl.BlockSpec((BLOCK, HEAD_DIM), lambda b, n: (jnp.clip(n + shift, 0, nb - 1), 0))

    kv = lambda shift, col: pl.BlockSpec((BLOCK, KV_WIDTH), rows(shift, col))
    cache = pl.BlockSpec((1, 1, past, KV_WIDTH), lambda b, n: (b, layer, 0, 0))
    args = [sink, proj, proj, proj, proj, proj, proj, proj, cache_k, cache_v,
            rope_cos, rope_sin, rope_cos, rope_sin, rope_cos, rope_sin]
    specs = [pl.BlockSpec(memory_space=pltpu.SMEM),
             pl.BlockSpec((BLOCK, ATTN_WIDTH), rows(0, 0)),
             kv(-1, kb), kv(0, kb), kv(1, kb),
             kv(-1, kb + 1), kv(0, kb + 1), kv(1, kb + 1),
             cache, cache,
             tab(0), tab(0), tab(-1), tab(-1), tab(1), tab(1)]
    body, aliases = _reuse(functools.partial(_lat_attn_kernel, layer=layer, n_blocks=nb), args, specs, [into])
    return pl.pallas_call(
        body,
        grid=(n_seq, nb),
        in_specs=specs,
        out_specs=pl.BlockSpec((BLOCK, ATTN_WIDTH), lambda b, n: (b0 + b * nb + n, 0)),
        out_shape=jax.ShapeDtypeStruct(into.shape, into.dtype),
        input_output_aliases=aliases,
        compiler_params=_cparams(("arbitrary", "arbitrary"), 32 * MIB),
        name="latent_attention",
    )(*args)


def _conv3(u, w):
    n_rows = u.shape[0]
    row = lax.broadcasted_iota(jnp.int32, u.shape, 0)
    before = jnp.where(row == 0, 0.0, pltpu.roll(u, 1, 0))
    after = jnp.where(row == n_rows - 1, 0.0, pltpu.roll(u, n_rows - 1, 0))
    return before * w[0:1] + u * w[1:2] + after * w[2:3]


def _short_conv_kernel(b_ref, c_ref, x_ref, w_ref, o_ref):
    o_ref[...] = (b_ref[...] * _conv3(c_ref[...] * x_ref[...], w_ref[0])).astype(o_ref.dtype)


def _short_conv(proj, conv_w, layer, col0, row0, n_seq, seq, tc, out_rows, into=None):
    cw = conv_w.shape[2]
    tc = _tile(cw, tc)
    nc = cw // tc
    cb, rb = col0 // tc, row0 // seq

    def part(p):
        return pl.BlockSpec((seq, tc), lambda s, c: (rb + s, cb + p * nc + c))

    args = [proj, proj, proj, conv_w]
    specs = [part(0), part(1), part(2), pl.BlockSpec((1, 3, tc), lambda s, c: (layer, 0, c))]
    body, aliases = _reuse(_short_conv_kernel, args, specs, [into] if into is not None else None)
    return pl.pallas_call(
        body,
        grid=(n_seq, nc),
        in_specs=specs,
        out_specs=pl.BlockSpec((seq, tc), lambda s, c: (rb + s, c)),
        out_shape=jax.ShapeDtypeStruct((out_rows, cw), BF16),
        input_output_aliases=aliases,
        compiler_params=_cparams(("arbitrary", "arbitrary"), 2 * seq * tc * 14 + 6 * seq * tc * 4 + 8 * MIB),
        name="short_conv",
    )(*args)


def _hyena_kernel(x0_ref, x1_ref, v_ref, w0_ref, w1_ref, wv_ref, skip_ref, kre_ref, kim_ref,
                  c_ref, s_ref, ct_ref, st_ref, o_ref, g32_ref, g16_ref, acc_ref, *, n_k, scale):
    kk = pl.program_id(2)

    @pl.when(kk == 0)
    def _():
        g = _conv3(x1_ref[...], w1_ref[0]) * _conv3(v_ref[...], wv_ref[0])
        g32_ref[...] = g
        g16_ref[...] = g.astype(BF16)
        acc_ref[...] = jnp.zeros_like(acc_ref)

    g16 = g16_ref[...]
    u_c = jnp.dot(c_ref[...], g16, preferred_element_type=F32)
    u_s = jnp.dot(s_ref[...], g16, preferred_element_type=F32)
    kre, kim = kre_ref[0], kim_ref[0]
    y_re = u_c * kre + u_s * kim
    y_im = u_c * kim - u_s * kre
    acc_ref[...] += (jnp.dot(ct_ref[...], y_re.astype(BF16), preferred_element_type=F32)
                     - jnp.dot(st_ref[...], y_im.astype(BF16), preferred_element_type=F32))

    @pl.when(kk == n_k - 1)
    def _():
        y = acc_ref[...] * scale + skip_ref[0] * g32_ref[...]
        o_ref[...] = (_conv3(x0_ref[...], w0_ref[0]) * y).astype(o_ref.dtype)


def _hyena(proj, hyena_conv_w, hyena_skip, k_re, k_im, tabs, layer, col0, row0, n_seq, seq, tc, kc, out_rows,
           into=None):
    hw = hyena_skip.shape[2]
    tc, kc = _tile(hw, tc), _tile(seq, kc)
    nc, n_k = hw // tc, seq // kc
    cb, rb = col0 // tc, row0 // seq
    cos_b, sin_b, cos_t, sin_t = tabs

    def part(p):
        return pl.BlockSpec((seq, tc), lambda s, c, k: (rb + s, cb + p * nc + c))

    def wpart(p):
        return pl.BlockSpec((1, 3, tc), lambda s, c, k: (layer, 0, p * nc + c))

    spec_k = pl.BlockSpec((1, kc, tc), lambda s, c, k: (layer, k, c))
    fwd = pl.BlockSpec((kc, seq), lambda s, c, k: (k, 0))
    inv = pl.BlockSpec((seq, kc), lambda s, c, k: (0, k))
    args = [proj, proj, proj, hyena_conv_w, hyena_conv_w, hyena_conv_w, hyena_skip, k_re, k_im,
            cos_b, sin_b, cos_t, sin_t]
    specs = [part(0), part(1), part(2), wpart(0), wpart(1), wpart(2),
             pl.BlockSpec((1, 1, tc), lambda s, c, k: (layer, 0, c)),
             spec_k, spec_k, fwd, fwd, inv, inv]
    body, aliases = _reuse(functools.partial(_hyena_kernel, n_k=n_k, scale=1.0 / seq), args, specs,
                           [into] if into is not None else None)
    return pl.pallas_call(
        body,
        grid=(n_seq, nc, n_k),
        in_specs=specs,
        out_specs=pl.BlockSpec((seq, tc), lambda s, c, k: (rb + s, c)),
        out_shape=jax.ShapeDtypeStruct((out_rows, hw), BF16),
        input_output_aliases=aliases,
        scratch_shapes=[pltpu.VMEM((seq, tc), F32), pltpu.VMEM((seq, tc), BF16), pltpu.VMEM((seq, tc), F32)],
        compiler_params=_cparams(("arbitrary", "arbitrary", "arbitrary"),
                                 2 * (3 * seq * tc * 4 + seq * tc * 2 + 2 * kc * tc * 4 + 4 * kc * seq * 2)
                                 + seq * tc * 10 + 6 * seq * tc * 4 + 8 * MIB),
        name="hyena",
    )(*args)


def _split_bf16(x):
    hi = x.astype(BF16)
    return hi, (x - hi.astype(F32)).astype(BF16)


def _dot_split(a_hi, a_lo, b_hi, b_lo):
    return (jnp.dot(a_hi, b_hi, preferred_element_type=F32) + jnp.dot(a_hi, b_lo, preferred_element_type=F32)
            + jnp.dot(a_lo, b_hi, preferred_element_type=F32))


def _filter_kernel(z_ref, w1_ref, b1_ref, fr_ref, w2_ref, b2_ref, w3f_ref, w3b_ref, decay_ref,
                   ch_ref, cl_ref, sh_ref, sl_ref, kre_ref, kim_ref, hph_ref, hpl_ref, hmh_ref, hml_ref):
    @pl.when(pl.program_id(2) == 0)
    def _():
        fr = fr_ref[0]
        hdn = jnp.sin(fr * (jnp.dot(z_ref[...], w1_ref[0], precision=HI, preferred_element_type=F32) + b1_ref[0]))
        hdn = jnp.sin(fr * (jnp.dot(hdn, w2_ref[0], precision=HI, preferred_element_type=F32) + b2_ref[0]))
        decay = decay_ref[...]
        h_fwd = jnp.dot(hdn, w3f_ref[0], precision=HI, preferred_element_type=F32) * decay
        h_bwd = jnp.dot(hdn, w3b_ref[0], precision=HI, preferred_element_type=F32) * decay
        row = lax.broadcasted_iota(jnp.int32, h_bwd.shape, 0)
        h_bwd = jnp.where(row == 0, 0.0, h_bwd)
        hph_ref[...], hpl_ref[...] = _split_bf16(h_fwd + h_bwd)
        hmh_ref[...], hml_ref[...] = _split_bf16(h_fwd - h_bwd)

    kre_ref[0] = _dot_split(ch_ref[...], cl_ref[...], hph_ref[...], hpl_ref[...])
    kim_ref[0] = -_dot_split(sh_ref[...], sl_ref[...], hmh_ref[...], hml_ref[...])


def _filter_spectra(z, w1, b1, fr, w2, b2, w3, decay, tabs, tc, kc):
    depth = w1.shape[0]
    seq, hw = decay.shape
    tc, kc = _tile(hw, tc), _tile(seq, kc)
    nc, n_k = hw // tc, seq // kc
    small = lambda shape: pl.BlockSpec((1,) + shape, lambda l, c, k: (l, 0, 0))
    out = pl.BlockSpec((1, kc, tc), lambda l, c, k: (l, k, c))
    tab = pl.BlockSpec((kc, seq), lambda l, c, k: (k, 0))
    return pl.pallas_call(
        _filter_kernel,
        grid=(depth, nc, n_k),
        in_specs=[pl.BlockSpec((seq, FILT_PAD), lambda l, c, k: (0, 0)),
                  small((FILT_PAD, FILT_PAD)), small((1, FILT_PAD)), small((1, FILT_PAD)),
                  small((FILT_PAD, FILT_PAD)), small((1, FILT_PAD)),
                  pl.BlockSpec((1, FILT_PAD, tc), lambda l, c, k: (l, 0, c)),
                  pl.BlockSpec((1, FILT_PAD, tc), lambda l, c, k: (l, 0, nc + c)),
                  pl.BlockSpec((seq, tc), lambda l, c, k: (0, c)),
                  tab, tab, tab, tab],
        out_specs=[out, out],
        out_shape=[jax.ShapeDtypeStruct((depth, seq, hw), F32)] * 2,
        scratch_shapes=[pltpu.VMEM((seq, tc), BF16)] * 4,
        compiler_params=_cparams(("arbitrary", "arbitrary", "arbitrary"),
                                 2 * (seq * tc * 4 + 4 * kc * seq * 2 + 2 * kc * tc * 4 + seq * FILT_PAD * 4)
                                 + 6 * seq * tc * 4 + 8 * MIB),
        name="filter_spectra",
    )(z, w1, b1, fr, w2, b2, w3, w3, decay, *tabs)


def _rope_tables(seq):
    rows = seq // GRID_W
    t_row = jnp.repeat(jnp.arange(rows, dtype=F32), GRID_W)
    t_col = jnp.tile(jnp.arange(GRID_W, dtype=F32), rows)
    inv = ROPE_BASE ** (-jnp.arange(0, AXIS_DIM, 2, dtype=F32) / AXIS_DIM)
    c_r, s_r = jnp.cos(t_row[:, None] * inv), jnp.sin(t_row[:, None] * inv)
    c_c, s_c = jnp.cos(t_col[:, None] * inv), jnp.sin(t_col[:, None] * inv)
    cos = jnp.concatenate([c_r, c_r, c_c, c_c], axis=-1)
    sin_signed = jnp.concatenate([-s_r, s_r, -s_c, s_c], axis=-1)
    return cos, sin_signed


def _filter_features(seq):
    pos = jnp.arange(seq, dtype=F32)
    t = jnp.linspace(0.0, 1.0, seq, dtype=F32)
    bands = (FILTER_EMB - 1) // 2
    f = jnp.linspace(1e-4, bands - 1, bands, dtype=F32)
    ang = 2.0 * math.pi * pos[:, None] * f[None, :] / seq
    z = jnp.concatenate([t[:, None], jnp.cos(ang), -jnp.sin(ang)], axis=-1)
    return jnp.pad(z, ((0, 0), (0, FILT_PAD - FILTER_EMB)))


def _filter_decay(seq, width):
    t = jnp.linspace(0.0, 1.0, seq, dtype=F32)
    deltas = jnp.abs(jnp.linspace(math.log(HYENA_TARGET) / SLOW_DECAY_PCT,
                                  math.log(HYENA_TARGET) / FAST_DECAY_PCT, width, dtype=F32))
    return jnp.exp(-t[:, None] * deltas[None, :])


def _odd_dft_tables(seq):
    k = jnp.arange(seq, dtype=jnp.int32)[:, None]

    def cos_sin(t):
        phase = ((2 * k + 1) * t) % (4 * seq)
        ang = phase.astype(F32) * (math.pi / (2 * seq))
        return jnp.cos(ang), jnp.sin(ang)

    assert seq % DFT_SPLIT == 0
    c_hi, s_hi = cos_sin(jnp.arange(0, seq, DFT_SPLIT, dtype=jnp.int32)[None, :])
    c_lo, s_lo = cos_sin(jnp.arange(DFT_SPLIT, dtype=jnp.int32)[None, :])
    cos = c_hi[:, :, None] * c_lo[:, None, :] - s_hi[:, :, None] * s_lo[:, None, :]
    sin = s_hi[:, :, None] * c_lo[:, None, :] + c_hi[:, :, None] * s_lo[:, None, :]
    return cos.reshape(seq, seq), sin.reshape(seq, seq)


def kernel(x_prompt, x_sample, cache_k, cache_v, c, c_ctx, norm_g, w_mod, b_mod, w_in, attn_sink, conv_w, hyena_conv_w,
           filt_w1, filt_b1, filt_freq, filt_w2, filt_b2, filt_w3, hyena_skip, w_branch, w_gate, b_gate, w_o, w_ffn_in,
           w_ffn_out):
    n_ctx, s_ctx, d = x_prompt.shape
    n_lat, s_lat, _ = x_sample.shape
    depth = w_in.shape[0]
    cw = conv_w.shape[2]
    hw = hyena_skip.shape[1]
    d_ff = w_ffn_out.shape[1]
    past = cache_k.shape[2]
    m_ctx, m_lat = n_ctx * s_ctx, n_lat * s_lat
    m = m_ctx + m_lat
    tm_row = s_ctx
    assert s_lat % tm_row == 0 and m_ctx % s_lat == 0 and n_lat + 1 <= MOD_ROWS
    col_conv = ATTN_WIDTH + 2 * KV_WIDTH
    col_hy = col_conv + 3 * cw

    def seq_of_tile(i):
        n_ctx_tiles, per_lat = m_ctx // tm_row, s_lat // tm_row
        return jnp.where(i < n_ctx_tiles, 0, 1 + (i - n_ctx_tiles) // per_lat)

    w_in_b = w_in[:1].astype(BF16)
    w_gate_b = w_gate[:1].astype(BF16)

    cvecs = jnp.concatenate([c_ctx[None], c, jnp.zeros((MOD_ROWS - 1 - n_lat, d), F32)], axis=0)
    mod_tab = _modulations(cvecs, w_mod, b_mod).reshape(depth * MOD_ROWS * N_MOD, 1, d)
    norm_tab = norm_g.reshape(depth * 4, 1, d)

    fp = FILT_PAD - filt_w1.shape[2]
    f_w1 = jnp.pad(filt_w1, ((0, 0), (0, FILT_PAD - FILTER_EMB), (0, fp)))
    f_b1 = jnp.pad(filt_b1, ((0, 0), (0, fp)))[:, None]
    f_fr = jnp.pad(filt_freq, ((0, 0), (0, fp)))[:, None]
    f_w2 = jnp.pad(filt_w2, ((0, 0), (0, fp), (0, fp)))
    f_b2 = jnp.pad(filt_b2, ((0, 0), (0, fp)))[:, None]
    f_w3 = jnp.pad(filt_w3, ((0, 0), (0, fp), (0, 0)))
    spectra, dft = {}, {}
    for seq in sorted({s_ctx, s_lat}):
        cos32, sin32 = _odd_dft_tables(seq)
        cos_b, sin_b = cos32.astype(BF16), sin32.astype(BF16)
        cos_lo, sin_lo = (cos32 - cos_b.astype(F32)).astype(BF16), (sin32 - sin_b.astype(F32)).astype(BF16)
        spectra[seq] = _filter_spectra(_filter_features(seq), f_w1, f_b1, f_fr, f_w2, f_b2, f_w3,
                                       _filter_decay(seq, hw), (cos_b, cos_lo, sin_b, sin_lo), 512, 256)
        dft[seq] = (cos_b, sin_b, cos_b.T, sin_b.T)

    rope_cos, rope_sin = _rope_tables(s_lat)
    cache_k2 = cache_k.reshape(n_lat, depth, past, KV_WIDTH)
    cache_v2 = cache_v.reshape(n_lat, depth, past, KV_WIDTH)
    b_gate3 = b_gate.reshape(depth, 1, -1)
    skip3 = hyena_skip.reshape(depth, 1, hw)

    groups = ((0, m_ctx), (m_ctx, m_lat))

    def res_stage(x_parts, y, split_out, **kw):
        if len(x_parts) == 1 and not split_out:
            return _resnorm(x_parts[0][0], y, norm_tab, mod_tab, seq_of_tile, tm_row, rows=(0, m), **kw)
        outs, per_group = None, []
        for gi, (g0, n_rows) in enumerate(groups):
            xa, xb = x_parts[gi] if len(x_parts) > 1 else x_parts[0]
            if split_out:
                per_group.append(_resnorm(xa, y, norm_tab, mod_tab, seq_of_tile, tm_row, rows=(g0, n_rows), x_base=xb,
                                          out_base=g0, **kw))
            else:
                outs = _resnorm(xa, y, norm_tab, mod_tab, seq_of_tile, tm_row, rows=(g0, n_rows), x_base=xb,
                                out_rows=m, into=outs, **kw)
        return per_group if split_out else outs

    x_parts = [(x_prompt.reshape(m_ctx, d), 0), (x_sample.reshape(m_lat, d), m_ctx)]
    (h,) = res_stage(x_parts, None, False, h_norm=0, h_scale=(0, 1), h_shift=(0, 0))
    new_kv = None
    for l in range(depth):
        proj, (w_branch_b,) = _matmul(h, w_in_b, 0, F32, 1024, 1024, [(w_branch, l)])
        gates, (w_o_b,) = _gates(h, w_gate_b, b_gate3, 0, l, 1024, 1024, [(w_o, l)])

        o_attn, new_kv = _ctx_attention(proj, attn_sink, l, n_ctx, s_ctx, m, depth, new_kv)
        o_attn = _lat_attention(proj, cache_k2, cache_v2, attn_sink, rope_cos, rope_sin, l, m_ctx, n_lat, s_lat, o_attn)
        o_conv = _short_conv(proj, conv_w, l, col_conv, 0, n_ctx, s_ctx, 1024, m)
        o_conv = _short_conv(proj, conv_w, l, col_conv, m_ctx, n_lat, s_lat, 256, m, o_conv)
        o_hy = _hyena(proj, hyena_conv_w, skip3, *spectra[s_ctx], dft[s_ctx], l, col_hy, 0, n_ctx, s_ctx, 1024, 512, m)
        o_hy = _hyena(proj, hyena_conv_w, skip3, *spectra[s_lat], dft[s_lat], l, col_hy, m_ctx, n_lat, s_lat, 512, 256,
                      m, o_hy)

        merged, (w_ffn_in_b,) = _merge(o_attn, o_conv, o_hy, w_branch_b[None], gates, 0, 1024, 512, [(w_ffn_in, l)])
        mix, _ = _matmul(merged, w_o_b[None], 0, BF16, 1024, 1024)
        x, h = res_stage(x_parts, mix, False, y_norm=l * 4 + 1, gate=(l, 2),
                         h_norm=l * 4 + 2, h_scale=(l, 4), h_shift=(l, 3))
        x_parts = [(x, 0)]
        next_casts = [(w_in, l + 1), (w_gate, l + 1)] if l + 1 < depth else []
        hidden, (w_ffn_out_b, *next_w) = _ffn_in(h, w_ffn_in_b[None], 0, 1024, 2, [(w_ffn_out, l)] + next_casts)
        if next_w:
            w_in_b, w_gate_b = next_w[0][None], next_w[1][None]
        f, _ = _matmul(hidden, w_ffn_out_b[None], 0, BF16, 512, 512)
        if l + 1 < depth:
            x, h = res_stage(x_parts, f, False, y_norm=l * 4 + 3, gate=(l, 5),
                             h_norm=(l + 1) * 4, h_scale=(l + 1, 1), h_shift=(l + 1, 0))
            x_parts = [(x, 0)]
        else:
            (y_prompt,), (y_sample,) = res_stage(x_parts, f, True, y_norm=l * 4 + 3, gate=(l, 5))

    return (y_prompt.reshape(n_ctx, s_ctx, d), y_sample.reshape(n_lat, s_lat, d),
            new_kv[0].reshape(n_ctx, depth, s_ctx, N_KV_HEADS, HEAD_DIM),
            new_kv[1].reshape(n_ctx, depth, s_ctx, N_KV_HEADS, HEAD_DIM))
```

```python
import functools
import math

import jax
import jax.numpy as jnp
from jax import lax
from jax.experimental import pallas as pl
from jax.experimental.pallas import tpu as pltpu

F32 = jnp.float32
BF16 = jnp.bfloat16

HEAD_DIM = 128
N_HEADS = 16
N_KV_HEADS = 4
GQA_GROUP = N_HEADS // N_KV_HEADS
ATTN_WIDTH = N_HEADS * HEAD_DIM
KV_WIDTH = N_KV_HEADS * HEAD_DIM
WINDOW = 128
BLOCK = 128
GRID_W = 64
ROPE_BASE = 10000.0
AXIS_DIM = HEAD_DIM // 2
AXIS_PAIRS = AXIS_DIM // 2
FILTER_EMB = 33
HYENA_TARGET = 1e-2
FAST_DECAY_PCT = 0.3
SLOW_DECAY_PCT = 1.5
EPS = 1e-6
SCALE = HEAD_DIM ** -0.5
LOG2E = math.log2(math.e)
NEG_INF = -1e30
N_MOD = 6
MOD_ROWS = 8
FILT_PAD = 128
DFT_SPLIT = 64

LANE = 128
MXU_COLS = 256
BF16_ROWS = 16
MIB = 1024 * 1024
VMEM_CAP_BYTES = 56 * MIB
HI = lax.Precision.HIGHEST


def _cparams(semantics, vmem_bytes):
    return pltpu.CompilerParams(dimension_semantics=semantics,
                                vmem_limit_bytes=int(min(max(vmem_bytes, 16 * MIB), VMEM_CAP_BYTES)))


def _tile(n, pref):
    if n <= pref:
        return n
    t = (pref // LANE) * LANE
    while t > LANE and n % t:
        t -= LANE
    assert n % t == 0, (n, pref)
    return t


def _skip_alias(body, n_in, n_alias, *refs):
    body(*refs[:n_in], *refs[n_in + n_alias:])


def _reuse(body, args, specs, into):
    if not into:
        return body, {}
    n_in = len(args)
    aliases = {n_in + k: k for k in range(len(into))}
    args += list(into)
    specs += [pl.BlockSpec(memory_space=pl.ANY)] * len(into)
    return functools.partial(_skip_alias, body, n_in, len(into)), aliases


def _mods_kernel(c_ref, w_ref, b_ref, o_ref):
    c = c_ref[...]
    a = (c * jax.nn.sigmoid(c)).astype(BF16)
    o_ref[0] = jnp.dot(a, w_ref[0].astype(BF16), preferred_element_type=F32) + b_ref[0]


def _modulations(cvecs, w_mod, b_mod):
    depth, d, n = w_mod.shape
    tn = _tile(n, 512)
    return pl.pallas_call(
        _mods_kernel,
        grid=(depth, n // tn),
        in_specs=[
            pl.BlockSpec((MOD_ROWS, d), lambda l, j: (0, 0)),
            pl.BlockSpec((1, d, tn), lambda l, j: (l, 0, j)),
            pl.BlockSpec((1, 1, tn), lambda l, j: (l, 0, j)),
        ],
        out_specs=pl.BlockSpec((1, MOD_ROWS, tn), lambda l, j: (l, 0, j)),
        out_shape=jax.ShapeDtypeStruct((depth, MOD_ROWS, n), F32),
        compiler_params=_cparams(("arbitrary", "arbitrary"), 2 * d * tn * 4 + 8 * MIB),
        name="modulations",
    )(cvecs, w_mod, b_mod.reshape(depth, 1, n))


def _rms(x):
    return x * lax.rsqrt(jnp.mean(x * x, axis=-1, keepdims=True) + EPS)


def _resnorm_kernel(*refs, n_terms, write_x, has_h):
    it = iter(refs)
    x_ref = next(it)
    terms = [(next(it), next(it), next(it)) for _ in range(n_terms)]
    if has_h:
        gnx_ref, sc_ref, sh_ref = next(it), next(it), next(it)
    if write_x:
        xo_ref = next(it)
    if has_h:
        h_ref = next(it)
    x = x_ref[...]
    for y_ref, gny_ref, gate_ref in terms:
        x = x + gate_ref[0] * (_rms(y_ref[...].astype(F32)) * gny_ref[0])
    if write_x:
        xo_ref[...] = x
    if has_h:
        h_ref[...] = ((_rms(x) * gnx_ref[0]) * (1.0 + sc_ref[0]) + sh_ref[0]).astype(h_ref.dtype)


def _resnorm(x, terms, norm_tab, mod_tab, seq_of_tile, tm, *, rows, write_x, x_base=0, out_base=0, out_rows=None,
             into=None, h_norm=None, h_scale=None, h_shift=None):
    g0, n_rows = rows
    d = x.shape[1]
    out_rows = n_rows if out_rows is None else out_rows
    has_h = h_norm is not None
    assert g0 % tm == 0 and n_rows % tm == 0 and x_base % tm == 0 and out_base % tm == 0
    xt, gt, ot = (g0 - x_base) // tm, g0 // tm, (g0 - out_base) // tm
    x_row = pl.BlockSpec((tm, d), lambda i: (i + xt, 0))
    y_row = pl.BlockSpec((tm, d), lambda i: (i + gt, 0))
    o_row = pl.BlockSpec((tm, d), lambda i: (i + ot, 0))

    def tab(idx):
        return pl.BlockSpec((1, 1, d), lambda i: (idx, 0, 0))

    def mod(layer_which):
        layer, which = layer_which
        return pl.BlockSpec((1, 1, d), lambda i: ((layer * MOD_ROWS + seq_of_tile(i + gt)) * N_MOD + which, 0, 0))

    args, specs = [x], [x_row]
    for y, y_norm, gate in terms:
        args += [y, norm_tab, mod_tab]
        specs += [y_row, tab(y_norm), mod(gate)]
    if has_h:
        args += [norm_tab, mod_tab, mod_tab]
        specs += [tab(h_norm), mod(h_scale), mod(h_shift)]
    out_shape, out_specs = [], []
    if write_x:
        out_shape.append(jax.ShapeDtypeStruct((out_rows, d), F32))
        out_specs.append(o_row)
    if has_h:
        out_shape.append(jax.ShapeDtypeStruct((out_rows, d), BF16))
        out_specs.append(o_row)
    body, aliases = _reuse(functools.partial(_resnorm_kernel, n_terms=len(terms), write_x=write_x, has_h=has_h),
                           args, specs, into)
    return pl.pallas_call(
        body,
        grid=(n_rows // tm,),
        in_specs=specs,
        out_specs=out_specs,
        out_shape=out_shape,
        input_output_aliases=aliases,
        compiler_params=_cparams(("arbitrary",), 2 * tm * d * (10 + 2 * len(terms)) + 3 * tm * d * 4 + 8 * MIB),
        name="resnorm",
    )(*args)


def _with_casts(body, n_in, n_cast, n_out, *refs):
    srcs = refs[n_in:n_in + n_cast]
    dsts = refs[n_in + n_cast + n_out:n_in + 2 * n_cast + n_out]
    body(*refs[:n_in], *refs[n_in + n_cast:n_in + n_cast + n_out], *refs[n_in + 2 * n_cast + n_out:])
    for src, dst in zip(srcs, dsts):
        dst[...] = src[...].astype(dst.dtype)


def _cast_jobs(casts, steps, step_of):
    args, in_specs, out_shapes, out_specs, vmem_bytes = [], [], [], [], 0
    for w, layer in casts:
        k, cols = w.shape[1], w.shape[2]
        rows = BF16_ROWS
        while k % rows or k // rows > steps:
            rows += BF16_ROWS
        n_blocks = k // rows
        blk = lambda *ids, n_blocks=n_blocks: jnp.minimum(step_of(*ids), n_blocks - 1)
        args.append(w.reshape(-1, cols))
        in_specs.append(pl.BlockSpec((rows, cols), lambda *ids, blk=blk, first=layer * n_blocks: (first + blk(*ids), 0)))
        out_shapes.append(jax.ShapeDtypeStruct((k, cols), BF16))
        out_specs.append(pl.BlockSpec((rows, cols), lambda *ids, blk=blk: (blk(*ids), 0)))
        vmem_bytes += 2 * rows * cols * 6
    return args, in_specs, out_shapes, out_specs, vmem_bytes


def _dense_call(body, args, specs, grid, out_shape, out_spec, vmem_bytes, name, casts=(), into=None):
    nj = grid[1]
    n_in = len(args)
    args, specs = list(args), list(specs)
    single = not isinstance(out_shape, (list, tuple))
    out_shapes, out_specs = ([out_shape], [out_spec]) if single else (list(out_shape), list(out_spec))
    n_out = len(out_shapes)
    c_args, c_specs, c_shapes, c_out_specs, c_vmem = _cast_jobs(casts, grid[0] * nj, lambda i, j: i * nj + j)
    args, specs, out_shapes, out_specs = args + c_args, specs + c_specs, out_shapes + c_shapes, out_specs + c_out_specs
    vmem_bytes += c_vmem
    kernel = functools.partial(_with_casts, body, n_in, len(casts), n_out)
    kernel, aliases = _reuse(kernel, args, specs, into)
    outs = pl.pallas_call(
        kernel, grid=grid, in_specs=specs, out_specs=out_specs, out_shape=out_shapes, input_output_aliases=aliases,
        compiler_params=_cparams(("arbitrary", "arbitrary"), vmem_bytes), name=name,
    )(*args)
    return (outs[0] if single else list(outs[:n_out])), list(outs[n_out:])


def _mm_kernel(a_ref, w_ref, o_ref):
    o_ref[...] = jnp.dot(a_ref[...], w_ref[0], preferred_element_type=F32).astype(o_ref.dtype)


def _matmul(a, w, layer, out_dtype, tm, tn, casts=()):
    m, k = a.shape
    n = w.shape[2]
    tm, tn = _tile(m, tm), _tile(n, tn)
    ob = jnp.dtype(out_dtype).itemsize
    return _dense_call(
        _mm_kernel, [a, w],
        [pl.BlockSpec((tm, k), lambda i, j: (i, 0)), pl.BlockSpec((1, k, tn), lambda i, j: (layer, 0, j))],
        (m // tm, n // tn), jax.ShapeDtypeStruct((m, n), out_dtype), pl.BlockSpec((tm, tn), lambda i, j: (i, j)),
        2 * (tm * k * 2 + k * tn * 2 + tm * tn * ob) + 2 * tm * tn * 4 + 8 * MIB, "matmul", casts)


def _sigmoid(x):
    return 0.5 * jnp.tanh(0.5 * x) + 0.5


def _gate_kernel(a_ref, w_ref, b_ref, o_ref):
    acc = jnp.dot(a_ref[...], w_ref[0], preferred_element_type=F32) + b_ref[0]
    o_ref[...] = _sigmoid(acc).astype(o_ref.dtype)


def _gates(h, w_gate, b_gate, layer, b_layer, tm, tn, casts=()):
    m, k = h.shape
    n = w_gate.shape[2]
    tm, tn = _tile(m, tm), _tile(n, tn)
    return _dense_call(
        _gate_kernel, [h, w_gate, b_gate],
        [pl.BlockSpec((tm, k), lambda i, j: (i, 0)),
         pl.BlockSpec((1, k, tn), lambda i, j: (layer, 0, j)),
         pl.BlockSpec((1, 1, tn), lambda i, j: (b_layer, 0, j))],
        (m // tm, n // tn), jax.ShapeDtypeStruct((m, n), BF16), pl.BlockSpec((tm, tn), lambda i, j: (i, j)),
        2 * (tm * k * 2 + k * tn * 2 + tm * tn * 2) + 2 * tm * tn * 4 + 8 * MIB, "gates", casts)


def _merge_kernel(oa_ref, oc_ref, oh_ref, wa_ref, wc_ref, wh_ref, ga_ref, gc_ref, gh_ref, o_ref):
    ba = jnp.dot(oa_ref[...], wa_ref[0], preferred_element_type=F32)
    bc = jnp.dot(oc_ref[...], wc_ref[0], preferred_element_type=F32)
    bh = jnp.dot(oh_ref[...], wh_ref[0], preferred_element_type=F32)
    merged = ga_ref[...].astype(F32) * ba + gc_ref[...].astype(F32) * bc + gh_ref[...].astype(F32) * bh
    o_ref[...] = merged.astype(o_ref.dtype)


def _merge(o_attn, o_conv, o_hy, w_branch, gates, layer, tm, tn, casts=()):
    m = o_attn.shape[0]
    d = w_branch.shape[2]
    cw = o_conv.shape[1]
    assert ATTN_WIDTH % cw == 0
    tm, tn = _tile(m, tm), _tile(d, tn)
    nj = d // tn
    conv_blk = ATTN_WIDTH // cw
    return _dense_call(
        _merge_kernel, [o_attn, o_conv, o_hy, w_branch, w_branch, w_branch, gates, gates, gates],
        [pl.BlockSpec((tm, ATTN_WIDTH), lambda i, j: (i, 0)),
         pl.BlockSpec((tm, cw), lambda i, j: (i, 0)),
         pl.BlockSpec((tm, cw), lambda i, j: (i, 0)),
         pl.BlockSpec((1, ATTN_WIDTH, tn), lambda i, j: (layer, 0, j)),
         pl.BlockSpec((1, cw, tn), lambda i, j: (layer, conv_blk, j)),
         pl.BlockSpec((1, cw, tn), lambda i, j: (layer, conv_blk + 1, j)),
         pl.BlockSpec((tm, tn), lambda i, j: (i, j)),
         pl.BlockSpec((tm, tn), lambda i, j: (i, nj + j)),
         pl.BlockSpec((tm, tn), lambda i, j: (i, 2 * nj + j))],
        (m // tm, nj), jax.ShapeDtypeStruct((m, d), BF16), pl.BlockSpec((tm, tn), lambda i, j: (i, j)),
        2 * (tm * (ATTN_WIDTH + 2 * cw) * 2 + (ATTN_WIDTH + 2 * cw) * tn * 2 + 4 * tm * tn * 2)
        + 3 * tm * tn * 4 + 8 * MIB, "merge", casts)


def _ffn_in_kernel(h_ref, *refs, n_sub):
    w_refs, o_ref = refs[:-1], refs[-1]
    h = h_ref[...]
    for s in range(n_sub):
        a = jnp.dot(h, w_refs[s][0], preferred_element_type=F32)
        b = jnp.dot(h, w_refs[n_sub + s][0], preferred_element_type=F32)
        o_ref[:, s * MXU_COLS:(s + 1) * MXU_COLS] = (a * _sigmoid(a) * b).astype(o_ref.dtype)


def _ffn_in(h, w_ffn_in, layer, tm, n_sub, casts=()):
    m, k = h.shape
    ff = w_ffn_in.shape[2] // 2
    assert ff % MXU_COLS == 0
    tm = _tile(m, tm)
    nb = ff // MXU_COLS

    def call(first, n_tiles, width, casts, into):
        assert first % width == 0
        tn = width * MXU_COLS
        w_spec = lambda half, s: pl.BlockSpec(
            (1, k, MXU_COLS), lambda i, j: (layer, 0, half * nb + first + width * j + s))
        return _dense_call(
            functools.partial(_ffn_in_kernel, n_sub=width), [h] + [w_ffn_in] * (2 * width),
            [pl.BlockSpec((tm, k), lambda i, j: (i, 0))]
            + [w_spec(0, s) for s in range(width)] + [w_spec(1, s) for s in range(width)],
            (m // tm, n_tiles), jax.ShapeDtypeStruct((m, ff), BF16),
            pl.BlockSpec((tm, tn), lambda i, j: (i, first // width + j)),
            2 * (tm * k * 2 + 2 * k * tn * 2 + tm * tn * 2) + 2 * tm * tn * 4 + 8 * MIB, "ffn_in", casts, into)

    hidden, cast_out = call(0, nb // n_sub, n_sub, casts, None)
    if nb % n_sub:
        hidden, _ = call(nb - nb % n_sub, 1, nb % n_sub, (), [hidden])
    return hidden, cast_out


def _group_attention(q_heads, keys, vals, sinks, bias):
    rows = q_heads[0].shape[0]
    q = jnp.concatenate([qh.astype(BF16) for qh in q_heads], axis=0)
    s = lax.dot_general(q, keys, (((1,), (1,)), ((), ())), preferred_element_type=F32)
    probs, denoms = [], []
    for g, sk in enumerate(sinks):
        sg = s[g * rows:(g + 1) * rows]
        if bias is not None:
            sg = sg + bias
        sink = sk * LOG2E
        m = jnp.maximum(jnp.max(sg, axis=-1, keepdims=True), sink)
        p = jnp.exp2(sg - m)
        denoms.append(jnp.sum(p, axis=-1, keepdims=True) + jnp.exp2(sink - m))
        probs.append(p.astype(BF16))
    o = jnp.dot(jnp.concatenate(probs, axis=0), vals, preferred_element_type=F32)
    return [o[g * rows:(g + 1) * rows] / denoms[g] for g in range(len(q_heads))]


def _ctx_attn_kernel(sink_ref, q_ref, k_ref, v_ref, new_k_ref, new_v_ref, o_ref, *, layer):
    new_k_ref[0, 0] = k_ref[...]
    new_v_ref[0, 0] = v_ref[...]
    for hk in range(N_KV_HEADS):
        cols = slice(hk * HEAD_DIM, (hk + 1) * HEAD_DIM)
        heads = range(hk * GQA_GROUP, (hk + 1) * GQA_GROUP)
        q_heads = [q_ref[:, h * HEAD_DIM:(h + 1) * HEAD_DIM] * (SCALE * LOG2E) for h in heads]
        outs = _group_attention(q_heads, k_ref[:, cols].astype(BF16), v_ref[:, cols].astype(BF16),
                                [sink_ref[layer, h] for h in heads], None)
        for h, o in zip(heads, outs):
            o_ref[:, h * HEAD_DIM:(h + 1) * HEAD_DIM] = o.astype(o_ref.dtype)


def _ctx_attention(proj, sink, layer, n_seq, seq, out_rows, depth, new_kv):
    kb = ATTN_WIDTH // KV_WIDTH
    kv_shape = jax.ShapeDtypeStruct((n_seq, depth, seq, KV_WIDTH), F32)
    kv_spec = pl.BlockSpec((1, 1, seq, KV_WIDTH), lambda b: (b, layer, 0, 0))
    args = [sink, proj, proj, proj]
    specs = [pl.BlockSpec(memory_space=pltpu.SMEM),
             pl.BlockSpec((seq, ATTN_WIDTH), lambda b: (b, 0)),
             pl.BlockSpec((seq, KV_WIDTH), lambda b: (b, kb)),
             pl.BlockSpec((seq, KV_WIDTH), lambda b: (b, kb + 1))]
    body, aliases = _reuse(functools.partial(_ctx_attn_kernel, layer=layer), args, specs, new_kv)
    new_k, new_v, o = pl.pallas_call(
        body,
        grid=(n_seq,),
        in_specs=specs,
        out_specs=[kv_spec, kv_spec, pl.BlockSpec((seq, ATTN_WIDTH), lambda b: (b, 0))],
        out_shape=[kv_shape, kv_shape, jax.ShapeDtypeStruct((out_rows, ATTN_WIDTH), BF16)],
        input_output_aliases=aliases,
        compiler_params=_cparams(("arbitrary",), 32 * MIB),
        name="ctx_attention",
    )(*args)
    return o, [new_k, new_v]


def _rope(x, cos, sin_signed, lo_half):
    partner = jnp.where(lo_half, pltpu.roll(x, HEAD_DIM - AXIS_PAIRS, 1), pltpu.roll(x, AXIS_PAIRS, 1))
    return x * cos + partner * sin_signed


def _lat_attn_kernel(sink_ref, q_ref, kp_ref, kc_ref, kn_ref, vp_ref, vc_ref, vn_ref, ck_ref, cv_ref,
                     cq_ref, sq_ref, cp_ref, sp_ref, cn_ref, sn_ref, o_ref, *, layer, n_blocks):
    n = pl.program_id(1)
    past = ck_ref.shape[2]
    nk = 3 * BLOCK + past
    lane = lax.broadcasted_iota(jnp.int32, (BLOCK, HEAD_DIM), 1)
    lo_half = (lane % AXIS_DIM) < AXIS_PAIRS
    qi = lax.broadcasted_iota(jnp.int32, (BLOCK, nk), 0)
    kj = lax.broadcasted_iota(jnp.int32, (BLOCK, nk), 1)
    first_ok = jnp.where(n > 0, qi, BLOCK)
    last_ok = jnp.where(n < n_blocks - 1, qi, -1)
    bias_prev = jnp.where(kj >= first_ok, 0.0, NEG_INF)
    bias_next = jnp.where(kj - 2 * BLOCK <= last_ok, 0.0, NEG_INF)
    bias = jnp.where(kj < BLOCK, bias_prev, jnp.where(kj < 2 * BLOCK, 0.0, jnp.where(kj < 3 * BLOCK, bias_next, 0.0)))
    cq, sq = cq_ref[...], sq_ref[...]
    cq_scaled, sq_scaled = cq * (SCALE * LOG2E), sq * (SCALE * LOG2E)
    for hk in range(N_KV_HEADS):
        cols = slice(hk * HEAD_DIM, (hk + 1) * HEAD_DIM)
        heads = range(hk * GQA_GROUP, (hk + 1) * GQA_GROUP)
        keys = jnp.concatenate([
            _rope(kp_ref[:, cols], cp_ref[...], sp_ref[...], lo_half).astype(BF16),
            _rope(kc_ref[:, cols], cq, sq, lo_half).astype(BF16),
            _rope(kn_ref[:, cols], cn_ref[...], sn_ref[...], lo_half).astype(BF16),
            ck_ref[0, 0, :, cols].astype(BF16)], axis=0)
        vals = jnp.concatenate([vp_ref[:, cols].astype(BF16), vc_ref[:, cols].astype(BF16),
                                vn_ref[:, cols].astype(BF16), cv_ref[0, 0, :, cols].astype(BF16)], axis=0)
        q_heads = [_rope(q_ref[:, h * HEAD_DIM:(h + 1) * HEAD_DIM], cq_scaled, sq_scaled, lo_half) for h in heads]
        outs = _group_attention(q_heads, keys, vals, [sink_ref[layer, h] for h in heads], bias)
        for h, o in zip(heads, outs):
            o_ref[:, h * HEAD_DIM:(h + 1) * HEAD_DIM] = o.astype(o_ref.dtype)


def _lat_attention(proj, cache_k, cache_v, sink, rope_cos, rope_sin, layer, row0, n_seq, seq, into):
    nb = seq // BLOCK
    kb = ATTN_WIDTH // KV_WIDTH
    b0 = row0 // BLOCK
    past = cache_k.shape[2]

    def blk(shift):
        return lambda b, n: (b0 + b * nb + jnp.clip(n + shift, 0, nb - 1))

    def rows(shift, col):
        f = blk(shift)
        return lambda b, n: (f(b, n), col)

    def tab(shift):
        return pl.BlockSpec((BLOCK, HEAD_DIM), lambda b, n: (jnp.clip(n + shift, 0, nb - 1), 0))

    kv = lambda shift, col: pl.BlockSpec((BLOCK, KV_WIDTH), rows(shift, col))
    cache = pl.BlockSpec((1, 1, past, KV_WIDTH), lambda b, n: (b, layer, 0, 0))
    args = [sink, proj, proj, proj, proj, proj, proj, proj, cache_k, cache_v,
            rope_cos, rope_sin, rope_cos, rope_sin, rope_cos, rope_sin]
    specs = [pl.BlockSpec(memory_space=pltpu.SMEM),
             pl.BlockSpec((BLOCK, ATTN_WIDTH), rows(0, 0)),
             kv(-1, kb), kv(0, kb), kv(1, kb),
             kv(-1, kb + 1), kv(0, kb + 1), kv(1, kb + 1),
             cache, cache,
             tab(0), tab(0), tab(-1), tab(-1), tab(1), tab(1)]
    body, aliases = _reuse(functools.partial(_lat_attn_kernel, layer=layer, n_blocks=nb), args, specs, [into])
    return pl.pallas_call(
        body,
        grid=(n_seq, nb),
        in_specs=specs,
        out_specs=pl.BlockSpec((BLOCK, ATTN_WIDTH), lambda b, n: (b0 + b * nb + n, 0)),
        out_shape=jax.ShapeDtypeStruct(into.shape, into.dtype),
        input_output_aliases=aliases,
        compiler_params=_cparams(("arbitrary", "arbitrary"), 32 * MIB),
        name="latent_attention",
    )(*args)


def _conv3(u, w):
    n_rows = u.shape[0]
    row = lax.broadcasted_iota(jnp.int32, u.shape, 0)
    before = jnp.where(row == 0, 0.0, pltpu.roll(u, 1, 0))
    after = jnp.where(row == n_rows - 1, 0.0, pltpu.roll(u, n_rows - 1, 0))
    return before * w[0:1] + u * w[1:2] + after * w[2:3]


def _short_conv_kernel(b_ref, c_ref, x_ref, w_ref, o_ref):
    o_ref[...] = (b_ref[...] * _conv3(c_ref[...] * x_ref[...], w_ref[0])).astype(o_ref.dtype)


def _short_conv(proj, conv_w, layer, col0, row0, n_seq, seq, tc, out_rows, into=None):
    cw = conv_w.shape[2]
    tc = _tile(cw, tc)
    nc = cw // tc
    cb, rb = col0 // tc, row0 // seq

    def part(p):
        return pl.BlockSpec((seq, tc), lambda s, c: (rb + s, cb + p * nc + c))

    args = [proj, proj, proj, conv_w]
    specs = [part(0), part(1), part(2), pl.BlockSpec((1, 3, tc), lambda s, c: (layer, 0, c))]
    body, aliases = _reuse(_short_conv_kernel, args, specs, [into] if into is not None else None)
    return pl.pallas_call(
        body,
        grid=(n_seq, nc),
        in_specs=specs,
        out_specs=pl.BlockSpec((seq, tc), lambda s, c: (rb + s, c)),
        out_shape=jax.ShapeDtypeStruct((out_rows, cw), BF16),
        input_output_aliases=aliases,
        compiler_params=_cparams(("arbitrary", "arbitrary"), 2 * seq * tc * 14 + 6 * seq * tc * 4 + 8 * MIB),
        name="short_conv",
    )(*args)


def _hyena_kernel(x0_ref, x1_ref, v_ref, w0_ref, w1_ref, wv_ref, skip_ref, kre_ref, kim_ref,
                  c_ref, s_ref, ct_ref, st_ref, o_ref, g32_ref, g16_ref, acc_ref, *, n_k, scale):
    kk = pl.program_id(2)

    @pl.when(kk == 0)
    def _():
        g = _conv3(x1_ref[...], w1_ref[0]) * _conv3(v_ref[...], wv_ref[0])
        g32_ref[...] = g
        g16_ref[...] = g.astype(BF16)
        acc_ref[...] = jnp.zeros_like(acc_ref)

    g16 = g16_ref[...]
    u_c = jnp.dot(c_ref[...], g16, preferred_element_type=F32)
    u_s = jnp.dot(s_ref[...], g16, preferred_element_type=F32)
    kre, kim = kre_ref[0], kim_ref[0]
    y_re = u_c * kre + u_s * kim
    y_im = u_c * kim - u_s * kre
    acc_ref[...] += (jnp.dot(ct_ref[...], y_re.astype(BF16), preferred_element_type=F32)
                     - jnp.dot(st_ref[...], y_im.astype(BF16), preferred_element_type=F32))

    @pl.when(kk == n_k - 1)
    def _():
        y = acc_ref[...] * scale + skip_ref[0] * g32_ref[...]
        o_ref[...] = (_conv3(x0_ref[...], w0_ref[0]) * y).astype(o_ref.dtype)


def _hyena(proj, hyena_conv_w, hyena_skip, k_re, k_im, tabs, layer, col0, row0, n_seq, seq, tc, kc, out_rows,
           into=None):
    hw = hyena_skip.shape[2]
    tc, kc = _tile(hw, tc), _tile(seq, kc)
    nc, n_k = hw // tc, seq // kc
    cb, rb = col0 // tc, row0 // seq
    cos_b, sin_b, cos_t, sin_t = tabs

    def part(p):
        return pl.BlockSpec((seq, tc), lambda s, c, k: (rb + s, cb + p * nc + c))

    def wpart(p):
        return pl.BlockSpec((1, 3, tc), lambda s, c, k: (layer, 0, p * nc + c))

    spec_k = pl.BlockSpec((1, kc, tc), lambda s, c, k: (layer, k, c))
    fwd = pl.BlockSpec((kc, seq), lambda s, c, k: (k, 0))
    inv = pl.BlockSpec((seq, kc), lambda s, c, k: (0, k))
    args = [proj, proj, proj, hyena_conv_w, hyena_conv_w, hyena_conv_w, hyena_skip, k_re, k_im,
            cos_b, sin_b, cos_t, sin_t]
    specs = [part(0), part(1), part(2), wpart(0), wpart(1), wpart(2),
             pl.BlockSpec((1, 1, tc), lambda s, c, k: (layer, 0, c)),
             spec_k, spec_k, fwd, fwd, inv, inv]
    body, aliases = _reuse(functools.partial(_hyena_kernel, n_k=n_k, scale=1.0 / seq), args, specs,
                           [into] if into is not None else None)
    return pl.pallas_call(
        body,
        grid=(n_seq, nc, n_k),
        in_specs=specs,
        out_specs=pl.BlockSpec((seq, tc), lambda s, c, k: (rb + s, c)),
        out_shape=jax.ShapeDtypeStruct((out_rows, hw), BF16),
        input_output_aliases=aliases,
        scratch_shapes=[pltpu.VMEM((seq, tc), F32), pltpu.VMEM((seq, tc), BF16), pltpu.VMEM((seq, tc), F32)],
        compiler_params=_cparams(("arbitrary", "arbitrary", "arbitrary"),
                                 2 * (3 * seq * tc * 4 + seq * tc * 2 + 2 * kc * tc * 4 + 4 * kc * seq * 2)
                                 + seq * tc * 10 + 6 * seq * tc * 4 + 8 * MIB),
        name="hyena",
    )(*args)


def _split_bf16(x):
    hi = x.astype(BF16)
    return hi, (x - hi.astype(F32)).astype(BF16)


def _dot_split(a_hi, a_lo, b_hi, b_lo):
    return (jnp.dot(a_hi, b_hi, preferred_element_type=F32) + jnp.dot(a_hi, b_lo, preferred_element_type=F32)
            + jnp.dot(a_lo, b_hi, preferred_element_type=F32))


def _filter_kernel(z_ref, w1_ref, b1_ref, fr_ref, w2_ref, b2_ref, w3f_ref, w3b_ref, decay_ref,
                   ch_ref, cl_ref, sh_ref, sl_ref, kre_ref, kim_ref, hph_ref, hpl_ref, hmh_ref, hml_ref):
    @pl.when(pl.program_id(2) == 0)
    def _():
        fr = fr_ref[0]
        hdn = jnp.sin(fr * (jnp.dot(z_ref[...], w1_ref[0], precision=HI, preferred_element_type=F32) + b1_ref[0]))
        hdn = jnp.sin(fr * (jnp.dot(hdn, w2_ref[0], precision=HI, preferred_element_type=F32) + b2_ref[0]))
        decay = decay_ref[...]
        h_fwd = jnp.dot(hdn, w3f_ref[0], precision=HI, preferred_element_type=F32) * decay
        h_bwd = jnp.dot(hdn, w3b_ref[0], precision=HI, preferred_element_type=F32) * decay
        row = lax.broadcasted_iota(jnp.int32, h_bwd.shape, 0)
        h_bwd = jnp.where(row == 0, 0.0, h_bwd)
        hph_ref[...], hpl_ref[...] = _split_bf16(h_fwd + h_bwd)
        hmh_ref[...], hml_ref[...] = _split_bf16(h_fwd - h_bwd)

    kre_ref[0] = _dot_split(ch_ref[...], cl_ref[...], hph_ref[...], hpl_ref[...])
    kim_ref[0] = -_dot_split(sh_ref[...], sl_ref[...], hmh_ref[...], hml_ref[...])


def _filter_spectra(z, w1, b1, fr, w2, b2, w3, decay, tabs, tc, kc, casts=()):
    depth = w1.shape[0]
    seq, hw = decay.shape
    tc, kc = _tile(hw, tc), _tile(seq, kc)
    nc, n_k = hw // tc, seq // kc
    small = lambda shape: pl.BlockSpec((1,) + shape, lambda l, c, k: (l, 0, 0))
    out = pl.BlockSpec((1, kc, tc), lambda l, c, k: (l, k, c))
    tab = pl.BlockSpec((kc, seq), lambda l, c, k: (k, 0))
    args = [z, w1, b1, fr, w2, b2, w3, w3, decay, *tabs]
    specs = [pl.BlockSpec((seq, FILT_PAD), lambda l, c, k: (0, 0)),
             small((FILT_PAD, FILT_PAD)), small((1, FILT_PAD)), small((1, FILT_PAD)),
             small((FILT_PAD, FILT_PAD)), small((1, FILT_PAD)),
             pl.BlockSpec((1, FILT_PAD, tc), lambda l, c, k: (l, 0, c)),
             pl.BlockSpec((1, FILT_PAD, tc), lambda l, c, k: (l, 0, nc + c)),
             pl.BlockSpec((seq, tc), lambda l, c, k: (0, c)),
             tab, tab, tab, tab]
    c_args, c_specs, c_shapes, c_out_specs, c_vmem = _cast_jobs(
        casts, depth * nc * n_k, lambda l, c, k: (l * nc + c) * n_k + k)
    outs = pl.pallas_call(
        functools.partial(_with_casts, _filter_kernel, len(args), len(casts), 2),
        grid=(depth, nc, n_k),
        in_specs=specs + c_specs,
        out_specs=[out, out] + c_out_specs,
        out_shape=[jax.ShapeDtypeStruct((depth, seq, hw), F32)] * 2 + c_shapes,
        scratch_shapes=[pltpu.VMEM((seq, tc), BF16)] * 4,
        compiler_params=_cparams(("arbitrary", "arbitrary", "arbitrary"),
                                 2 * (seq * tc * 4 + 4 * kc * seq * 2 + 2 * kc * tc * 4 + seq * FILT_PAD * 4)
                                 + 6 * seq * tc * 4 + 8 * MIB + c_vmem),
        name="filter_spectra",
    )(*args, *c_args)
    return list(outs[:2]), list(outs[2:])


def _rope_tables(seq):
    rows = seq // GRID_W
    t_row = jnp.repeat(jnp.arange(rows, dtype=F32), GRID_W)
    t_col = jnp.tile(jnp.arange(GRID_W, dtype=F32), rows)
    inv = ROPE_BASE ** (-jnp.arange(0, AXIS_DIM, 2, dtype=F32) / AXIS_DIM)
    c_r, s_r = jnp.cos(t_row[:, None] * inv), jnp.sin(t_row[:, None] * inv)
    c_c, s_c = jnp.cos(t_col[:, None] * inv), jnp.sin(t_col[:, None] * inv)
    cos = jnp.concatenate([c_r, c_r, c_c, c_c], axis=-1)
    sin_signed = jnp.concatenate([-s_r, s_r, -s_c, s_c], axis=-1)
    return cos, sin_signed


def _filter_features(seq):
    pos = jnp.arange(seq, dtype=F32)
    t = jnp.linspace(0.0, 1.0, seq, dtype=F32)
    bands = (FILTER_EMB - 1) // 2
    f = jnp.linspace(1e-4, bands - 1, bands, dtype=F32)
    ang = 2.0 * math.pi * pos[:, None] * f[None, :] / seq
    z = jnp.concatenate([t[:, None], jnp.cos(ang), -jnp.sin(ang)], axis=-1)
    return jnp.pad(z, ((0, 0), (0, FILT_PAD - FILTER_EMB)))


def _filter_decay(seq, width):
    t = jnp.linspace(0.0, 1.0, seq, dtype=F32)
    deltas = jnp.abs(jnp.linspace(math.log(HYENA_TARGET) / SLOW_DECAY_PCT,
                                  math.log(HYENA_TARGET) / FAST_DECAY_PCT, width, dtype=F32))
    return jnp.exp(-t[:, None] * deltas[None, :])


def _odd_dft_tables(seq):
    k = jnp.arange(seq, dtype=jnp.int32)[:, None]

    def cos_sin(t):
        phase = ((2 * k + 1) * t) % (4 * seq)
        ang = phase.astype(F32) * (math.pi / (2 * seq))
        return jnp.cos(ang), jnp.sin(ang)

    assert seq % DFT_SPLIT == 0
    c_hi, s_hi = cos_sin(jnp.arange(0, seq, DFT_SPLIT, dtype=jnp.int32)[None, :])
    c_lo, s_lo = cos_sin(jnp.arange(DFT_SPLIT, dtype=jnp.int32)[None, :])
    cos = c_hi[:, :, None] * c_lo[:, None, :] - s_hi[:, :, None] * s_lo[:, None, :]
    sin = s_hi[:, :, None] * c_lo[:, None, :] + c_hi[:, :, None] * s_lo[:, None, :]
    return cos.reshape(seq, seq), sin.reshape(seq, seq)


def kernel(x_prompt, x_sample, cache_k, cache_v, c, c_ctx, norm_g, w_mod, b_mod, w_in, attn_sink, conv_w, hyena_conv_w,
           filt_w1, filt_b1, filt_freq, filt_w2, filt_b2, filt_w3, hyena_skip, w_branch, w_gate, b_gate, w_o, w_ffn_in,
           w_ffn_out):
    n_ctx, s_ctx, d = x_prompt.shape
    n_lat, s_lat, _ = x_sample.shape
    depth = w_in.shape[0]
    cw = conv_w.shape[2]
    hw = hyena_skip.shape[1]
    d_ff = w_ffn_out.shape[1]
    past = cache_k.shape[2]
    m_ctx, m_lat = n_ctx * s_ctx, n_lat * s_lat
    m = m_ctx + m_lat
    tm_row = s_ctx
    assert s_lat % tm_row == 0 and m_ctx % s_lat == 0 and n_lat + 1 <= MOD_ROWS
    col_conv = ATTN_WIDTH + 2 * KV_WIDTH
    col_hy = col_conv + 3 * cw

    def seq_of_tile(i):
        n_ctx_tiles, per_lat = m_ctx // tm_row, s_lat // tm_row
        return jnp.where(i < n_ctx_tiles, 0, 1 + (i - n_ctx_tiles) // per_lat)


    cvecs = jnp.concatenate([c_ctx[None], c, jnp.zeros((MOD_ROWS - 1 - n_lat, d), F32)], axis=0)
    mod_tab = _modulations(cvecs, w_mod, b_mod).reshape(depth * MOD_ROWS * N_MOD, 1, d)
    norm_tab = norm_g.reshape(depth * 4, 1, d)

    fp = FILT_PAD - filt_w1.shape[2]
    f_w1 = jnp.pad(filt_w1, ((0, 0), (0, FILT_PAD - FILTER_EMB), (0, fp)))
    f_b1 = jnp.pad(filt_b1, ((0, 0), (0, fp)))[:, None]
    f_fr = jnp.pad(filt_freq, ((0, 0), (0, fp)))[:, None]
    f_w2 = jnp.pad(filt_w2, ((0, 0), (0, fp), (0, fp)))
    f_b2 = jnp.pad(filt_b2, ((0, 0), (0, fp)))[:, None]
    f_w3 = jnp.pad(filt_w3, ((0, 0), (0, fp), (0, 0)))
    spectra, dft = {}, {}
    for seq in sorted({s_ctx, s_lat}):
        cos32, sin32 = _odd_dft_tables(seq)
        cos_b, sin_b = cos32.astype(BF16), sin32.astype(BF16)
        cos_lo, sin_lo = (cos32 - cos_b.astype(F32)).astype(BF16), (sin32 - sin_b.astype(F32)).astype(BF16)
        first_casts = [(w_in, 0), (w_gate, 0)] if seq == max(s_ctx, s_lat) else []
        spectra[seq], first_w = _filter_spectra(_filter_features(seq), f_w1, f_b1, f_fr, f_w2, f_b2, f_w3,
                                                _filter_decay(seq, hw), (cos_b, cos_lo, sin_b, sin_lo), 512, 256,
                                                first_casts)
        if first_w:
            w_in_b, w_gate_b = first_w[0][None], first_w[1][None]
        dft[seq] = (cos_b, sin_b, cos_b.T, sin_b.T)

    rope_cos, rope_sin = _rope_tables(s_lat)
    cache_k2 = cache_k.reshape(n_lat, depth, past, KV_WIDTH)
    cache_v2 = cache_v.reshape(n_lat, depth, past, KV_WIDTH)
    b_gate3 = b_gate.reshape(depth, 1, -1)
    skip3 = hyena_skip.reshape(depth, 1, hw)

    groups = ((0, m_ctx), (m_ctx, m_lat))

    def res_stage(x_parts, terms, split_out, **kw):
        if len(x_parts) == 1 and not split_out:
            return _resnorm(x_parts[0][0], terms, norm_tab, mod_tab, seq_of_tile, tm_row, rows=(0, m), **kw)
        outs, per_group = None, []
        for gi, (g0, n_rows) in enumerate(groups):
            xa, xb = x_parts[gi] if len(x_parts) > 1 else x_parts[0]
            if split_out:
                per_group.append(_resnorm(xa, terms, norm_tab, mod_tab, seq_of_tile, tm_row, rows=(g0, n_rows),
                                          x_base=xb, out_base=g0, **kw))
            else:
                outs = _resnorm(xa, terms, norm_tab, mod_tab, seq_of_tile, tm_row, rows=(g0, n_rows), x_base=xb,
                                out_rows=m, into=outs, **kw)
        return per_group if split_out else outs

    x_parts = [(x_prompt.reshape(m_ctx, d), 0), (x_sample.reshape(m_lat, d), m_ctx)]
    (h,) = res_stage(x_parts, [], False, write_x=False, h_norm=0, h_scale=(0, 1), h_shift=(0, 0))
    new_kv = None
    for l in range(depth):
        proj, (w_branch_b,) = _matmul(h, w_in_b, 0, F32, 1024, 1024, [(w_branch, l)])
        gates, (w_o_b,) = _gates(h, w_gate_b, b_gate3, 0, l, 1024, 1024, [(w_o, l)])

        o_attn, new_kv = _ctx_attention(proj, attn_sink, l, n_ctx, s_ctx, m, depth, new_kv)
        o_attn = _lat_attention(proj, cache_k2, cache_v2, attn_sink, rope_cos, rope_sin, l, m_ctx, n_lat, s_lat, o_attn)
        o_conv = _short_conv(proj, conv_w, l, col_conv, 0, n_ctx, s_ctx, 1024, m)
        o_conv = _short_conv(proj, conv_w, l, col_conv, m_ctx, n_lat, s_lat, 256, m, o_conv)
        o_hy = _hyena(proj, hyena_conv_w, skip3, *spectra[s_ctx], dft[s_ctx], l, col_hy, 0, n_ctx, s_ctx, 1024, 512, m)
        o_hy = _hyena(proj, hyena_conv_w, skip3, *spectra[s_lat], dft[s_lat], l, col_hy, m_ctx, n_lat, s_lat, 512, 256,
                      m, o_hy)

        merged, (w_ffn_in_b,) = _merge(o_attn, o_conv, o_hy, w_branch_b[None], gates, 0, 1024, 512, [(w_ffn_in, l)])
        mix, _ = _matmul(merged, w_o_b[None], 0, BF16, 1024, 1024)
        mixer_term = (mix, l * 4 + 1, (l, 2))
        (h,) = res_stage(x_parts, [mixer_term], False, write_x=False,
                         h_norm=l * 4 + 2, h_scale=(l, 4), h_shift=(l, 3))
        next_casts = [(w_in, l + 1), (w_gate, l + 1)] if l + 1 < depth else []
        hidden, (w_ffn_out_b, *next_w) = _ffn_in(h, w_ffn_in_b[None], 0, 1024, 2, [(w_ffn_out, l)] + next_casts)
        if next_w:
            w_in_b, w_gate_b = next_w[0][None], next_w[1][None]
        f, _ = _matmul(hidden, w_ffn_out_b[None], 0, BF16, 512, 512)
        layer_terms = [mixer_term, (f, l * 4 + 3, (l, 5))]
        if l + 1 < depth:
            x, h = res_stage(x_parts, layer_terms, False, write_x=True,
                             h_norm=(l + 1) * 4, h_scale=(l + 1, 1), h_shift=(l + 1, 0))
            x_parts = [(x, 0)]
        else:
            (y_prompt,), (y_sample,) = res_stage(x_parts, layer_terms, True, write_x=True)

    return (y_prompt.reshape(n_ctx, s_ctx, d), y_sample.reshape(n_lat, s_lat, d),
            new_kv[0].reshape(n_ctx, depth, s_ctx, N_KV_HEADS, HEAD_DIM),
            new_kv[1].reshape(n_ctx, depth, s_ctx, N_KV_HEADS, HEAD_DIM))
```

```python
import functools
import math

import jax
import jax.numpy as jnp
from jax import lax
from jax.experimental import pallas as pl
from jax.experimental.pallas import tpu as pltpu

F32 = jnp.float32
BF16 = jnp.bfloat16

HEAD_DIM = 128
N_HEADS = 16
N_KV_HEADS = 4
GQA_GROUP = N_HEADS // N_KV_HEADS
ATTN_WIDTH = N_HEADS * HEAD_DIM
KV_WIDTH = N_KV_HEADS * HEAD_DIM
WINDOW = 128
BLOCK = 128
GRID_W = 64
ROPE_BASE = 10000.0
AXIS_DIM = HEAD_DIM // 2
AXIS_PAIRS = AXIS_DIM // 2
FILTER_EMB = 33
HYENA_TARGET = 1e-2
FAST_DECAY_PCT = 0.3
SLOW_DECAY_PCT = 1.5
EPS = 1e-6
SCALE = HEAD_DIM ** -0.5
LOG2E = math.log2(math.e)
NEG_INF = -1e30
N_MOD = 6
MOD_ROWS = 8
FILT_PAD = 128
DFT_SPLIT = 64

LANE = 128
MXU_COLS = 256
BF16_ROWS = 16
MIB = 1024 * 1024
VMEM_CAP_BYTES = 56 * MIB
HI = lax.Precision.HIGHEST


def _cparams(semantics, vmem_bytes):
    return pltpu.CompilerParams(dimension_semantics=semantics,
                                vmem_limit_bytes=int(min(max(vmem_bytes, 16 * MIB), VMEM_CAP_BYTES)))


def _tile(n, pref):
    if n <= pref:
        return n
    t = (pref // LANE) * LANE
    while t > LANE and n % t:
        t -= LANE
    assert n % t == 0, (n, pref)
    return t


def _skip_alias(body, n_in, n_alias, *refs):
    body(*refs[:n_in], *refs[n_in + n_alias:])


def _reuse(body, args, specs, into):
    if not into:
        return body, {}
    n_in = len(args)
    aliases = {n_in + k: k for k in range(len(into))}
    args += list(into)
    specs += [pl.BlockSpec(memory_space=pl.ANY)] * len(into)
    return functools.partial(_skip_alias, body, n_in, len(into)), aliases


def _mods_kernel(c_ref, w_ref, b_ref, o_ref):
    c = c_ref[...]
    a = (c * jax.nn.sigmoid(c)).astype(BF16)
    o_ref[0] = jnp.dot(a, w_ref[0].astype(BF16), preferred_element_type=F32) + b_ref[0]


def _modulations(cvecs, w_mod, b_mod):
    depth, d, n = w_mod.shape
    tn = _tile(n, 512)
    return pl.pallas_call(
        _mods_kernel,
        grid=(depth, n // tn),
        in_specs=[
            pl.BlockSpec((MOD_ROWS, d), lambda l, j: (0, 0)),
            pl.BlockSpec((1, d, tn), lambda l, j: (l, 0, j)),
            pl.BlockSpec((1, 1, tn), lambda l, j: (l, 0, j)),
        ],
        out_specs=pl.BlockSpec((1, MOD_ROWS, tn), lambda l, j: (l, 0, j)),
        out_shape=jax.ShapeDtypeStruct((depth, MOD_ROWS, n), F32),
        compiler_params=_cparams(("arbitrary", "arbitrary"), 2 * d * tn * 4 + 8 * MIB),
        name="modulations",
    )(cvecs, w_mod, b_mod.reshape(depth, 1, n))


def _rms(x):
    return x * lax.rsqrt(jnp.mean(x * x, axis=-1, keepdims=True) + EPS)


def _resnorm_kernel(*refs, n_terms, write_x, has_h):
    it = iter(refs)
    x_ref = next(it)
    terms = [(next(it), next(it), next(it)) for _ in range(n_terms)]
    if has_h:
        gnx_ref, sc_ref, sh_ref = next(it), next(it), next(it)
    if write_x:
        xo_ref = next(it)
    if has_h:
        h_ref = next(it)
    x = x_ref[...]
    for y_ref, gny_ref, gate_ref in terms:
        x = x + _rms(y_ref[...].astype(F32)) * (gate_ref[0] * gny_ref[0])
    if write_x:
        xo_ref[...] = x
    if has_h:
        h_ref[...] = (_rms(x) * (gnx_ref[0] * (1.0 + sc_ref[0])) + sh_ref[0]).astype(h_ref.dtype)


def _resnorm(x, terms, norm_tab, mod_tab, seq_of_tile, tm, *, rows, write_x, x_base=0, out_base=0, out_rows=None,
             into=None, h_norm=None, h_scale=None, h_shift=None):
    g0, n_rows = rows
    d = x.shape[1]
    out_rows = n_rows if out_rows is None else out_rows
    has_h = h_norm is not None
    assert g0 % tm == 0 and n_rows % tm == 0 and x_base % tm == 0 and out_base % tm == 0
    xt, gt, ot = (g0 - x_base) // tm, g0 // tm, (g0 - out_base) // tm
    x_row = pl.BlockSpec((tm, d), lambda i: (i + xt, 0))
    y_row = pl.BlockSpec((tm, d), lambda i: (i + gt, 0))
    o_row = pl.BlockSpec((tm, d), lambda i: (i + ot, 0))

    def tab(idx):
        return pl.BlockSpec((1, 1, d), lambda i: (idx, 0, 0))

    def mod(layer_which):
        layer, which = layer_which
        return pl.BlockSpec((1, 1, d), lambda i: ((layer * MOD_ROWS + seq_of_tile(i + gt)) * N_MOD + which, 0, 0))

    args, specs = [x], [x_row]
    for y, y_norm, gate in terms:
        args += [y, norm_tab, mod_tab]
        specs += [y_row, tab(y_norm), mod(gate)]
    if has_h:
        args += [norm_tab, mod_tab, mod_tab]
        specs += [tab(h_norm), mod(h_scale), mod(h_shift)]
    out_shape, out_specs = [], []
    if write_x:
        out_shape.append(jax.ShapeDtypeStruct((out_rows, d), F32))
        out_specs.append(o_row)
    if has_h:
        out_shape.append(jax.ShapeDtypeStruct((out_rows, d), BF16))
        out_specs.append(o_row)
    body, aliases = _reuse(functools.partial(_resnorm_kernel, n_terms=len(terms), write_x=write_x, has_h=has_h),
                           args, specs, into)
    return pl.pallas_call(
        body,
        grid=(n_rows // tm,),
        in_specs=specs,
        out_specs=out_specs,
        out_shape=out_shape,
        input_output_aliases=aliases,
        compiler_params=_cparams(("arbitrary",),
                                 2 * tm * d * (4 + 2 * len(terms) + 4 * write_x + 2 * has_h) + 3 * tm * d * 4 + 8 * MIB),
        name="resnorm",
    )(*args)


def _with_casts(body, n_in, n_cast, n_out, *refs):
    srcs = refs[n_in:n_in + n_cast]
    dsts = refs[n_in + n_cast + n_out:n_in + 2 * n_cast + n_out]
    body(*refs[:n_in], *refs[n_in + n_cast:n_in + n_cast + n_out], *refs[n_in + 2 * n_cast + n_out:])
    for src, dst in zip(srcs, dsts):
        dst[...] = src[...].astype(dst.dtype)


def _cast_jobs(casts, steps, step_of):
    args, in_specs, out_shapes, out_specs, vmem_bytes = [], [], [], [], 0
    for w, layer in casts:
        k, cols = w.shape[1], w.shape[2]
        rows = BF16_ROWS
        while k % rows or k // rows > steps:
            rows += BF16_ROWS
        n_blocks = k // rows
        blk = lambda *ids, n_blocks=n_blocks: jnp.minimum(step_of(*ids), n_blocks - 1)
        args.append(w.reshape(-1, cols))
        in_specs.append(pl.BlockSpec((rows, cols), lambda *ids, blk=blk, first=layer * n_blocks: (first + blk(*ids), 0)))
        out_shapes.append(jax.ShapeDtypeStruct((k, cols), BF16))
        out_specs.append(pl.BlockSpec((rows, cols), lambda *ids, blk=blk: (blk(*ids), 0)))
        vmem_bytes += 2 * rows * cols * 6
    return args, in_specs, out_shapes, out_specs, vmem_bytes


def _dense_call(body, args, specs, grid, out_shape, out_spec, vmem_bytes, name, casts=(), into=None):
    nj = grid[1]
    n_in = len(args)
    args, specs = list(args), list(specs)
    single = not isinstance(out_shape, (list, tuple))
    out_shapes, out_specs = ([out_shape], [out_spec]) if single else (list(out_shape), list(out_spec))
    n_out = len(out_shapes)
    c_args, c_specs, c_shapes, c_out_specs, c_vmem = _cast_jobs(casts, grid[0] * nj, lambda i, j: i * nj + j)
    args, specs, out_shapes, out_specs = args + c_args, specs + c_specs, out_shapes + c_shapes, out_specs + c_out_specs
    vmem_bytes += c_vmem
    kernel = functools.partial(_with_casts, body, n_in, len(casts), n_out)
    kernel, aliases = _reuse(kernel, args, specs, into)
    outs = pl.pallas_call(
        kernel, grid=grid, in_specs=specs, out_specs=out_specs, out_shape=out_shapes, input_output_aliases=aliases,
        compiler_params=_cparams(("arbitrary", "arbitrary"), vmem_bytes), name=name,
    )(*args)
    return (outs[0] if single else list(outs[:n_out])), list(outs[n_out:])


def _mm_kernel(a_ref, w_ref, o_ref):
    o_ref[...] = jnp.dot(a_ref[...], w_ref[0], preferred_element_type=F32).astype(o_ref.dtype)


def _matmul(a, w, layer, out_dtype, tm, tn, casts=()):
    m, k = a.shape
    n = w.shape[2]
    tm, tn = _tile(m, tm), _tile(n, tn)
    ob = jnp.dtype(out_dtype).itemsize
    return _dense_call(
        _mm_kernel, [a, w],
        [pl.BlockSpec((tm, k), lambda i, j: (i, 0)), pl.BlockSpec((1, k, tn), lambda i, j: (layer, 0, j))],
        (m // tm, n // tn), jax.ShapeDtypeStruct((m, n), out_dtype), pl.BlockSpec((tm, tn), lambda i, j: (i, j)),
        2 * (tm * k * 2 + k * tn * 2 + tm * tn * ob) + 2 * tm * tn * 4 + 8 * MIB, "matmul", casts)


def _sigmoid(x):
    return 0.5 * jnp.tanh(0.5 * x) + 0.5


def _gate_kernel(a_ref, w_ref, b_ref, o_ref):
    acc = jnp.dot(a_ref[...], w_ref[0], preferred_element_type=F32) + b_ref[0]
    o_ref[...] = _sigmoid(acc).astype(o_ref.dtype)


def _gates(h, w_gate, b_gate, layer, b_layer, tm, tn, casts=()):
    m, k = h.shape
    n = w_gate.shape[2]
    tm, tn = _tile(m, tm), _tile(n, tn)
    return _dense_call(
        _gate_kernel, [h, w_gate, b_gate],
        [pl.BlockSpec((tm, k), lambda i, j: (i, 0)),
         pl.BlockSpec((1, k, tn), lambda i, j: (layer, 0, j)),
         pl.BlockSpec((1, 1, tn), lambda i, j: (b_layer, 0, j))],
        (m // tm, n // tn), jax.ShapeDtypeStruct((m, n), BF16), pl.BlockSpec((tm, tn), lambda i, j: (i, j)),
        2 * (tm * k * 2 + k * tn * 2 + tm * tn * 2) + 2 * tm * tn * 4 + 8 * MIB, "gates", casts)


def _merge_kernel(oa_ref, oc_ref, oh_ref, wa_ref, wc_ref, wh_ref, ga_ref, gc_ref, gh_ref, o_ref):
    ba = jnp.dot(oa_ref[...], wa_ref[0], preferred_element_type=F32)
    bc = jnp.dot(oc_ref[...], wc_ref[0], preferred_element_type=F32)
    bh = jnp.dot(oh_ref[...], wh_ref[0], preferred_element_type=F32)
    merged = ga_ref[...].astype(F32) * ba + gc_ref[...].astype(F32) * bc + gh_ref[...].astype(F32) * bh
    o_ref[...] = merged.astype(o_ref.dtype)


def _merge(o_attn, o_conv, o_hy, w_branch, gates, layer, tm, tn, casts=()):
    m = o_attn.shape[0]
    d = w_branch.shape[2]
    cw = o_conv.shape[1]
    assert ATTN_WIDTH % cw == 0
    tm, tn = _tile(m, tm), _tile(d, tn)
    nj = d // tn
    conv_blk = ATTN_WIDTH // cw
    return _dense_call(
        _merge_kernel, [o_attn, o_conv, o_hy, w_branch, w_branch, w_branch, gates, gates, gates],
        [pl.BlockSpec((tm, ATTN_WIDTH), lambda i, j: (i, 0)),
         pl.BlockSpec((tm, cw), lambda i, j: (i, 0)),
         pl.BlockSpec((tm, cw), lambda i, j: (i, 0)),
         pl.BlockSpec((1, ATTN_WIDTH, tn), lambda i, j: (layer, 0, j)),
         pl.BlockSpec((1, cw, tn), lambda i, j: (layer, conv_blk, j)),
         pl.BlockSpec((1, cw, tn), lambda i, j: (layer, conv_blk + 1, j)),
         pl.BlockSpec((tm, tn), lambda i, j: (i, j)),
         pl.BlockSpec((tm, tn), lambda i, j: (i, nj + j)),
         pl.BlockSpec((tm, tn), lambda i, j: (i, 2 * nj + j))],
        (m // tm, nj), jax.ShapeDtypeStruct((m, d), BF16), pl.BlockSpec((tm, tn), lambda i, j: (i, j)),
        2 * (tm * (ATTN_WIDTH + 2 * cw) * 2 + (ATTN_WIDTH + 2 * cw) * tn * 2 + 4 * tm * tn * 2)
        + 3 * tm * tn * 4 + 8 * MIB, "merge", casts)


def _ffn_in_kernel(h_ref, *refs, n_sub):
    w_refs, o_ref = refs[:-1], refs[-1]
    h = h_ref[...]
    for s in range(n_sub):
        a = jnp.dot(h, w_refs[s][0], preferred_element_type=F32)
        b = jnp.dot(h, w_refs[n_sub + s][0], preferred_element_type=F32)
        o_ref[:, s * MXU_COLS:(s + 1) * MXU_COLS] = (a * _sigmoid(a) * b).astype(o_ref.dtype)


def _ffn_in(h, w_ffn_in, layer, tm, n_sub, casts=()):
    m, k = h.shape
    ff = w_ffn_in.shape[2] // 2
    assert ff % MXU_COLS == 0
    tm = _tile(m, tm)
    nb = ff // MXU_COLS

    def call(first, n_tiles, width, casts, into):
        assert first % width == 0
        tn = width * MXU_COLS
        w_spec = lambda half, s: pl.BlockSpec(
            (1, k, MXU_COLS), lambda i, j: (layer, 0, half * nb + first + width * j + s))
        return _dense_call(
            functools.partial(_ffn_in_kernel, n_sub=width), [h] + [w_ffn_in] * (2 * width),
            [pl.BlockSpec((tm, k), lambda i, j: (i, 0))]
            + [w_spec(0, s) for s in range(width)] + [w_spec(1, s) for s in range(width)],
            (m // tm, n_tiles), jax.ShapeDtypeStruct((m, ff), BF16),
            pl.BlockSpec((tm, tn), lambda i, j: (i, first // width + j)),
            2 * (tm * k * 2 + 2 * k * tn * 2 + tm * tn * 2) + 2 * tm * tn * 4 + 8 * MIB, "ffn_in", casts, into)

    hidden, cast_out = call(0, nb // n_sub, n_sub, casts, None)
    if nb % n_sub:
        hidden, _ = call(nb - nb % n_sub, 1, nb % n_sub, (), [hidden])
    return hidden, cast_out


def _group_attention(q_heads, keys, vals, sinks, bias):
    rows = q_heads[0].shape[0]
    q = jnp.concatenate([qh.astype(BF16) for qh in q_heads], axis=0)
    s = lax.dot_general(q, keys, (((1,), (1,)), ((), ())), preferred_element_type=F32)
    probs, denoms = [], []
    for g, sk in enumerate(sinks):
        sg = s[g * rows:(g + 1) * rows]
        if bias is not None:
            sg = sg + bias
        sink = sk * LOG2E
        m = jnp.maximum(jnp.max(sg, axis=-1, keepdims=True), sink)
        p = jnp.exp2(sg - m)
        denoms.append(jnp.sum(p, axis=-1, keepdims=True) + jnp.exp2(sink - m))
        probs.append(p.astype(BF16))
    o = jnp.dot(jnp.concatenate(probs, axis=0), vals, preferred_element_type=F32)
    return [o[g * rows:(g + 1) * rows] / denoms[g] for g in range(len(q_heads))]


def _ctx_attn_kernel(sink_ref, q_ref, k_ref, v_ref, new_k_ref, new_v_ref, o_ref, *, layer):
    new_k_ref[0, 0] = k_ref[...]
    new_v_ref[0, 0] = v_ref[...]
    for hk in range(N_KV_HEADS):
        cols = slice(hk * HEAD_DIM, (hk + 1) * HEAD_DIM)
        heads = range(hk * GQA_GROUP, (hk + 1) * GQA_GROUP)
        q_heads = [q_ref[:, h * HEAD_DIM:(h + 1) * HEAD_DIM] * (SCALE * LOG2E) for h in heads]
        outs = _group_attention(q_heads, k_ref[:, cols].astype(BF16), v_ref[:, cols].astype(BF16),
                                [sink_ref[layer, h] for h in heads], None)
        for h, o in zip(heads, outs):
            o_ref[:, h * HEAD_DIM:(h + 1) * HEAD_DIM] = o.astype(o_ref.dtype)


def _ctx_attention(proj, sink, layer, n_seq, seq, out_rows, depth, new_kv):
    kb = ATTN_WIDTH // KV_WIDTH
    kv_shape = jax.ShapeDtypeStruct((n_seq, depth, seq, KV_WIDTH), F32)
    kv_spec = pl.BlockSpec((1, 1, seq, KV_WIDTH), lambda b: (b, layer, 0, 0))
    args = [sink, proj, proj, proj]
    specs = [pl.BlockSpec(memory_space=pltpu.SMEM),
             pl.BlockSpec((seq, ATTN_WIDTH), lambda b: (b, 0)),
             pl.BlockSpec((seq, KV_WIDTH), lambda b: (b, kb)),
             pl.BlockSpec((seq, KV_WIDTH), lambda b: (b, kb + 1))]
    body, aliases = _reuse(functools.partial(_ctx_attn_kernel, layer=layer), args, specs, new_kv)
    new_k, new_v, o = pl.pallas_call(
        body,
        grid=(n_seq,),
        in_specs=specs,
        out_specs=[kv_spec, kv_spec, pl.BlockSpec((seq, ATTN_WIDTH), lambda b: (b, 0))],
        out_shape=[kv_shape, kv_shape, jax.ShapeDtypeStruct((out_rows, ATTN_WIDTH), BF16)],
        input_output_aliases=aliases,
        compiler_params=_cparams(("arbitrary",), 32 * MIB),
        name="ctx_attention",
    )(*args)
    return o, [new_k, new_v]


def _rope(x, cos, sin_signed, lo_half):
    partner = jnp.where(lo_half, pltpu.roll(x, HEAD_DIM - AXIS_PAIRS, 1), pltpu.roll(x, AXIS_PAIRS, 1))
    return x * cos + partner * sin_signed


def _lat_attn_kernel(sink_ref, q_ref, kp_ref, kc_ref, kn_ref, vp_ref, vc_ref, vn_ref, ck_ref, cv_ref,
                     cq_ref, sq_ref, cp_ref, sp_ref, cn_ref, sn_ref, o_ref, *, layer, n_blocks):
    n = pl.program_id(1)
    past = ck_ref.shape[2]
    nk = 3 * BLOCK + past
    lane = lax.broadcasted_iota(jnp.int32, (BLOCK, HEAD_DIM), 1)
    lo_half = (lane % AXIS_DIM) < AXIS_PAIRS
    qi = lax.broadcasted_iota(jnp.int32, (BLOCK, nk), 0)
    kj = lax.broadcasted_iota(jnp.int32, (BLOCK, nk), 1)
    first_ok = jnp.where(n > 0, qi, BLOCK)
    last_ok = jnp.where(n < n_blocks - 1, qi, -1)
    bias_prev = jnp.where(kj >= first_ok, 0.0, NEG_INF)
    bias_next = jnp.where(kj - 2 * BLOCK <= last_ok, 0.0, NEG_INF)
    bias = jnp.where(kj < BLOCK, bias_prev, jnp.where(kj < 2 * BLOCK, 0.0, jnp.where(kj < 3 * BLOCK, bias_next, 0.0)))
    cq, sq = cq_ref[...], sq_ref[...]
    cq_scaled, sq_scaled = cq * (SCALE * LOG2E), sq * (SCALE * LOG2E)
    for hk in range(N_KV_HEADS):
        cols = slice(hk * HEAD_DIM, (hk + 1) * HEAD_DIM)
        heads = range(hk * GQA_GROUP, (hk + 1) * GQA_GROUP)
        keys = jnp.concatenate([
            _rope(kp_ref[:, cols], cp_ref[...], sp_ref[...], lo_half).astype(BF16),
            _rope(kc_ref[:, cols], cq, sq, lo_half).astype(BF16),
            _rope(kn_ref[:, cols], cn_ref[...], sn_ref[...], lo_half).astype(BF16),
            ck_ref[0, 0, :, cols].astype(BF16)], axis=0)
        vals = jnp.concatenate([vp_ref[:, cols].astype(BF16), vc_ref[:, cols].astype(BF16),
                                vn_ref[:, cols].astype(BF16), cv_ref[0, 0, :, cols].astype(BF16)], axis=0)
        q_heads = [_rope(q_ref[:, h * HEAD_DIM:(h + 1) * HEAD_DIM], cq_scaled, sq_scaled, lo_half) for h in heads]
        outs = _group_attention(q_heads, keys, vals, [sink_ref[layer, h] for h in heads], bias)
        for h, o in zip(heads, outs):
            o_ref[:, h * HEAD_DIM:(h + 1) * HEAD_DIM] = o.astype(o_ref.dtype)


def _lat_attention(proj, cache_k, cache_v, sink, rope_cos, rope_sin, layer, row0, n_seq, seq, into):
    assert WINDOW == BLOCK and seq % BLOCK == 0 and row0 % BLOCK == 0
    nb = seq // BLOCK
    kb = ATTN_WIDTH // KV_WIDTH
    b0 = row0 // BLOCK
    past = cache_k.shape[2]

    def blk(shift):
        return lambda b, n: (b0 + b * nb + jnp.clip(n + shift, 0, nb - 1))

    def rows(shift, col):
        f = blk(shift)
        return lambda b, n: (f(b, n), col)

    def tab(shift):
        return pl.BlockSpec((BLOCK, HEAD_DIM), lambda b, n: (jnp.clip(n + shift, 0, nb - 1), 0))

    kv = lambda shift, col: pl.BlockSpec((BLOCK, KV_WIDTH), rows(shift, col))
    cache = pl.BlockSpec((1, 1, past, KV_WIDTH), lambda b, n: (b, layer, 0, 0))
    args = [sink, proj, proj, proj, proj, proj, proj, proj, cache_k, cache_v,
            rope_cos, rope_sin, rope_cos, rope_sin, rope_cos, rope_sin]
    specs = [pl.BlockSpec(memory_space=pltpu.SMEM),
             pl.BlockSpec((BLOCK, ATTN_WIDTH), rows(0, 0)),
             kv(-1, kb), kv(0, kb), kv(1, kb),
             kv(-1, kb + 1), kv(0, kb + 1), kv(1, kb + 1),
             cache, cache,
             tab(0), tab(0), tab(-1), tab(-1), tab(1), tab(1)]
    body, aliases = _reuse(functools.partial(_lat_attn_kernel, layer=layer, n_blocks=nb), args, specs, [into])
    return pl.pallas_call(
        body,
        grid=(n_seq, nb),
        in_specs=specs,
        out_specs=pl.BlockSpec((BLOCK, ATTN_WIDTH), lambda b, n: (b0 + b * nb + n, 0)),
        out_shape=jax.ShapeDtypeStruct(into.shape, into.dtype),
        input_output_aliases=aliases,
        compiler_params=_cparams(("arbitrary", "arbitrary"), 32 * MIB),
        name="latent_attention",
    )(*args)


def _conv3(u, w):
    n_rows = u.shape[0]
    row = lax.broadcasted_iota(jnp.int32, u.shape, 0)
    before = jnp.where(row == 0, 0.0, pltpu.roll(u, 1, 0))
    after = jnp.where(row == n_rows - 1, 0.0, pltpu.roll(u, n_rows - 1, 0))
    return before * w[0:1] + u * w[1:2] + after * w[2:3]


def _short_conv_kernel(b_ref, c_ref, x_ref, w_ref, o_ref):
    o_ref[...] = (b_ref[...] * _conv3(c_ref[...] * x_ref[...], w_ref[0])).astype(o_ref.dtype)


def _short_conv(proj, conv_w, layer, col0, row0, n_seq, seq, tc, out_rows, into=None):
    cw = conv_w.shape[2]
    tc = _tile(cw, tc)
    nc = cw // tc
    cb, rb = col0 // tc, row0 // seq

    def part(p):
        return pl.BlockSpec((seq, tc), lambda s, c: (rb + s, cb + p * nc + c))

    args = [proj, proj, proj, conv_w]
    specs = [part(0), part(1), part(2), pl.BlockSpec((1, 3, tc), lambda s, c: (layer, 0, c))]
    body, aliases = _reuse(_short_conv_kernel, args, specs, [into] if into is not None else None)
    return pl.pallas_call(
        body,
        grid=(n_seq, nc),
        in_specs=specs,
        out_specs=pl.BlockSpec((seq, tc), lambda s, c: (rb + s, c)),
        out_shape=jax.ShapeDtypeStruct((out_rows, cw), BF16),
        input_output_aliases=aliases,
        compiler_params=_cparams(("arbitrary", "arbitrary"), 2 * seq * tc * 14 + 6 * seq * tc * 4 + 8 * MIB),
        name="short_conv",
    )(*args)


def _hyena_kernel(x0_ref, x1_ref, v_ref, w0_ref, w1_ref, wv_ref, skip_ref, kre_ref, kim_ref,
                  c_ref, s_ref, ct_ref, st_ref, o_ref, g32_ref, g16_ref, acc_ref, *, n_k, scale):
    kk = pl.program_id(2)

    @pl.when(kk == 0)
    def _():
        g = _conv3(x1_ref[...], w1_ref[0]) * _conv3(v_ref[...], wv_ref[0])
        g32_ref[...] = g
        g16_ref[...] = g.astype(BF16)
        acc_ref[...] = jnp.zeros_like(acc_ref)

    g16 = g16_ref[...]
    u_c = jnp.dot(c_ref[...], g16, preferred_element_type=F32)
    u_s = jnp.dot(s_ref[...], g16, preferred_element_type=F32)
    kre, kim = kre_ref[0], kim_ref[0]
    y_re = u_c * kre + u_s * kim
    y_im = u_c * kim - u_s * kre
    acc_ref[...] += (jnp.dot(ct_ref[...], y_re.astype(BF16), preferred_element_type=F32)
                     - jnp.dot(st_ref[...], y_im.astype(BF16), preferred_element_type=F32))

    @pl.when(kk == n_k - 1)
    def _():
        y = acc_ref[...] * scale + skip_ref[0] * g32_ref[...]
        o_ref[...] = (_conv3(x0_ref[...], w0_ref[0]) * y).astype(o_ref.dtype)


def _hyena(proj, hyena_conv_w, hyena_skip, k_re, k_im, tabs, layer, col0, row0, n_seq, seq, tc, kc, out_rows,
           into=None):
    hw = hyena_skip.shape[2]
    tc, kc = _tile(hw, tc), _tile(seq, kc)
    nc, n_k = hw // tc, seq // kc
    cb, rb = col0 // tc, row0 // seq
    cos_b, sin_b, cos_t, sin_t = tabs

    def part(p):
        return pl.BlockSpec((seq, tc), lambda s, c, k: (rb + s, cb + p * nc + c))

    def wpart(p):
        return pl.BlockSpec((1, 3, tc), lambda s, c, k: (layer, 0, p * nc + c))

    spec_k = pl.BlockSpec((1, kc, tc), lambda s, c, k: (layer, k, c))
    fwd = pl.BlockSpec((kc, seq), lambda s, c, k: (k, 0))
    inv = pl.BlockSpec((seq, kc), lambda s, c, k: (0, k))
    args = [proj, proj, proj, hyena_conv_w, hyena_conv_w, hyena_conv_w, hyena_skip, k_re, k_im,
            cos_b, sin_b, cos_t, sin_t]
    specs = [part(0), part(1), part(2), wpart(0), wpart(1), wpart(2),
             pl.BlockSpec((1, 1, tc), lambda s, c, k: (layer, 0, c)),
             spec_k, spec_k, fwd, fwd, inv, inv]
    body, aliases = _reuse(functools.partial(_hyena_kernel, n_k=n_k, scale=1.0 / seq), args, specs,
                           [into] if into is not None else None)
    return pl.pallas_call(
        body,
        grid=(n_seq, nc, n_k),
        in_specs=specs,
        out_specs=pl.BlockSpec((seq, tc), lambda s, c, k: (rb + s, c)),
        out_shape=jax.ShapeDtypeStruct((out_rows, hw), BF16),
        input_output_aliases=aliases,
        scratch_shapes=[pltpu.VMEM((seq, tc), F32), pltpu.VMEM((seq, tc), BF16), pltpu.VMEM((seq, tc), F32)],
        compiler_params=_cparams(("arbitrary", "arbitrary", "arbitrary"),
                                 2 * (3 * seq * tc * 4 + seq * tc * 2 + 2 * kc * tc * 4 + 4 * kc * seq * 2)
                                 + seq * tc * 10 + 6 * seq * tc * 4 + 8 * MIB),
        name="hyena",
    )(*args)


def _split_bf16(x):
    hi = x.astype(BF16)
    return hi, (x - hi.astype(F32)).astype(BF16)


def _dot_split(a_hi, a_lo, b_hi, b_lo):
    return (jnp.dot(a_hi, b_hi, preferred_element_type=F32) + jnp.dot(a_hi, b_lo, preferred_element_type=F32)
            + jnp.dot(a_lo, b_hi, preferred_element_type=F32))


def _filter_kernel(z_ref, w1_ref, b1_ref, fr_ref, w2_ref, b2_ref, w3f_ref, w3b_ref, decay_ref,
                   ch_ref, cl_ref, sh_ref, sl_ref, kre_ref, kim_ref, hph_ref, hpl_ref, hmh_ref, hml_ref):
    @pl.when(pl.program_id(2) == 0)
    def _():
        fr = fr_ref[0]
        hdn = jnp.sin(fr * (jnp.dot(z_ref[...], w1_ref[0], precision=HI, preferred_element_type=F32) + b1_ref[0]))
        hdn = jnp.sin(fr * (jnp.dot(hdn, w2_ref[0], precision=HI, preferred_element_type=F32) + b2_ref[0]))
        decay = decay_ref[...]
        h_fwd = jnp.dot(hdn, w3f_ref[0], precision=HI, preferred_element_type=F32) * decay
        h_bwd = jnp.dot(hdn, w3b_ref[0], precision=HI, preferred_element_type=F32) * decay
        row = lax.broadcasted_iota(jnp.int32, h_bwd.shape, 0)
        h_bwd = jnp.where(row == 0, 0.0, h_bwd)
        hph_ref[...], hpl_ref[...] = _split_bf16(h_fwd + h_bwd)
        hmh_ref[...], hml_ref[...] = _split_bf16(h_fwd - h_bwd)

    kre_ref[0] = _dot_split(ch_ref[...], cl_ref[...], hph_ref[...], hpl_ref[...])
    kim_ref[0] = -_dot_split(sh_ref[...], sl_ref[...], hmh_ref[...], hml_ref[...])


def _filter_spectra(z, w1, b1, fr, w2, b2, w3, decay, tabs, tc, kc, casts=()):
    depth = w1.shape[0]
    seq, hw = decay.shape
    tc, kc = _tile(hw, tc), _tile(seq, kc)
    nc, n_k = hw // tc, seq // kc
    small = lambda shape: pl.BlockSpec((1,) + shape, lambda l, c, k: (l, 0, 0))
    out = pl.BlockSpec((1, kc, tc), lambda l, c, k: (l, k, c))
    tab = pl.BlockSpec((kc, seq), lambda l, c, k: (k, 0))
    args = [z, w1, b1, fr, w2, b2, w3, w3, decay, *tabs]
    specs = [pl.BlockSpec((seq, FILT_PAD), lambda l, c, k: (0, 0)),
             small((FILT_PAD, FILT_PAD)), small((1, FILT_PAD)), small((1, FILT_PAD)),
             small((FILT_PAD, FILT_PAD)), small((1, FILT_PAD)),
             pl.BlockSpec((1, FILT_PAD, tc), lambda l, c, k: (l, 0, c)),
             pl.BlockSpec((1, FILT_PAD, tc), lambda l, c, k: (l, 0, nc + c)),
             pl.BlockSpec((seq, tc), lambda l, c, k: (0, c)),
             tab, tab, tab, tab]
    c_args, c_specs, c_shapes, c_out_specs, c_vmem = _cast_jobs(
        casts, depth * nc * n_k, lambda l, c, k: (l * nc + c) * n_k + k)
    outs = pl.pallas_call(
        functools.partial(_with_casts, _filter_kernel, len(args), len(casts), 2),
        grid=(depth, nc, n_k),
        in_specs=specs + c_specs,
        out_specs=[out, out] + c_out_specs,
        out_shape=[jax.ShapeDtypeStruct((depth, seq, hw), F32)] * 2 + c_shapes,
        scratch_shapes=[pltpu.VMEM((seq, tc), BF16)] * 4,
        compiler_params=_cparams(("arbitrary", "arbitrary", "arbitrary"),
                                 2 * (seq * tc * 4 + 4 * kc * seq * 2 + 2 * kc * tc * 4 + seq * FILT_PAD * 4)
                                 + 6 * seq * tc * 4 + 8 * MIB + c_vmem),
        name="filter_spectra",
    )(*args, *c_args)
    return list(outs[:2]), list(outs[2:])


def _rope_tables(seq):
    rows = seq // GRID_W
    t_row = jnp.repeat(jnp.arange(rows, dtype=F32), GRID_W)
    t_col = jnp.tile(jnp.arange(GRID_W, dtype=F32), rows)
    inv = ROPE_BASE ** (-jnp.arange(0, AXIS_DIM, 2, dtype=F32) / AXIS_DIM)
    c_r, s_r = jnp.cos(t_row[:, None] * inv), jnp.sin(t_row[:, None] * inv)
    c_c, s_c = jnp.cos(t_col[:, None] * inv), jnp.sin(t_col[:, None] * inv)
    cos = jnp.concatenate([c_r, c_r, c_c, c_c], axis=-1)
    sin_signed = jnp.concatenate([-s_r, s_r, -s_c, s_c], axis=-1)
    return cos, sin_signed


def _filter_features(seq):
    pos = jnp.arange(seq, dtype=F32)
    t = jnp.linspace(0.0, 1.0, seq, dtype=F32)
    bands = (FILTER_EMB - 1) // 2
    f = jnp.linspace(1e-4, bands - 1, bands, dtype=F32)
    ang = 2.0 * math.pi * pos[:, None] * f[None, :] / seq
    z = jnp.concatenate([t[:, None], jnp.cos(ang), -jnp.sin(ang)], axis=-1)
    return jnp.pad(z, ((0, 0), (0, FILT_PAD - FILTER_EMB)))


def _filter_decay(seq, width):
    t = jnp.linspace(0.0, 1.0, seq, dtype=F32)
    deltas = jnp.abs(jnp.linspace(math.log(HYENA_TARGET) / SLOW_DECAY_PCT,
                                  math.log(HYENA_TARGET) / FAST_DECAY_PCT, width, dtype=F32))
    return jnp.exp(-t[:, None] * deltas[None, :])


def _odd_dft_tables(seq):
    k = jnp.arange(seq, dtype=jnp.int32)[:, None]

    def cos_sin(t):
        phase = ((2 * k + 1) * t) % (4 * seq)
        ang = phase.astype(F32) * (math.pi / (2 * seq))
        return jnp.cos(ang), jnp.sin(ang)

    assert seq % DFT_SPLIT == 0
    c_hi, s_hi = cos_sin(jnp.arange(0, seq, DFT_SPLIT, dtype=jnp.int32)[None, :])
    c_lo, s_lo = cos_sin(jnp.arange(DFT_SPLIT, dtype=jnp.int32)[None, :])
    cos = c_hi[:, :, None] * c_lo[:, None, :] - s_hi[:, :, None] * s_lo[:, None, :]
    sin = s_hi[:, :, None] * c_lo[:, None, :] + c_hi[:, :, None] * s_lo[:, None, :]
    return cos.reshape(seq, seq), sin.reshape(seq, seq)


def kernel(x_prompt, x_sample, cache_k, cache_v, c, c_ctx, norm_g, w_mod, b_mod, w_in, attn_sink, conv_w, hyena_conv_w,
           filt_w1, filt_b1, filt_freq, filt_w2, filt_b2, filt_w3, hyena_skip, w_branch, w_gate, b_gate, w_o, w_ffn_in,
           w_ffn_out):
    n_ctx, s_ctx, d = x_prompt.shape
    n_lat, s_lat, _ = x_sample.shape
    depth = w_in.shape[0]
    cw = conv_w.shape[2]
    hw = hyena_skip.shape[1]
    d_ff = w_ffn_out.shape[1]
    past = cache_k.shape[2]
    m_ctx, m_lat = n_ctx * s_ctx, n_lat * s_lat
    m = m_ctx + m_lat
    assert s_lat % s_ctx == 0 and m_ctx % s_lat == 0 and n_lat + 1 <= MOD_ROWS
    col_conv = ATTN_WIDTH + 2 * KV_WIDTH
    col_hy = col_conv + 3 * cw

    def norm_tile(terms, write_x):
        wide = 2 * s_ctx
        light = len(terms) <= 1 and not write_x
        return wide if light and s_lat % wide == 0 and m_ctx % wide == 0 else s_ctx

    def seq_of_tile(tm):
        n_ctx_tiles, per_lat = m_ctx // tm, s_lat // tm
        return lambda i: jnp.where(i < n_ctx_tiles, 0, 1 + (i - n_ctx_tiles) // per_lat)


    cvecs = jnp.concatenate([c_ctx[None], c, jnp.zeros((MOD_ROWS - 1 - n_lat, d), F32)], axis=0)
    mod_tab = _modulations(cvecs, w_mod, b_mod).reshape(depth * MOD_ROWS * N_MOD, 1, d)
    norm_tab = norm_g.reshape(depth * 4, 1, d)

    fp = FILT_PAD - filt_w1.shape[2]
    f_w1 = jnp.pad(filt_w1, ((0, 0), (0, FILT_PAD - FILTER_EMB), (0, fp)))
    f_b1 = jnp.pad(filt_b1, ((0, 0), (0, fp)))[:, None]
    f_fr = jnp.pad(filt_freq, ((0, 0), (0, fp)))[:, None]
    f_w2 = jnp.pad(filt_w2, ((0, 0), (0, fp), (0, fp)))
    f_b2 = jnp.pad(filt_b2, ((0, 0), (0, fp)))[:, None]
    f_w3 = jnp.pad(filt_w3, ((0, 0), (0, fp), (0, 0)))
    spectra, dft = {}, {}
    for seq in sorted({s_ctx, s_lat}):
        cos32, sin32 = _odd_dft_tables(seq)
        cos_b, sin_b = cos32.astype(BF16), sin32.astype(BF16)
        cos_lo, sin_lo = (cos32 - cos_b.astype(F32)).astype(BF16), (sin32 - sin_b.astype(F32)).astype(BF16)
        first_casts = [(w_in, 0), (w_gate, 0)] if seq == max(s_ctx, s_lat) else []
        spectra[seq], first_w = _filter_spectra(_filter_features(seq), f_w1, f_b1, f_fr, f_w2, f_b2, f_w3,
                                                _filter_decay(seq, hw), (cos_b, cos_lo, sin_b, sin_lo), 512, 256,
                                                first_casts)
        if first_w:
            w_in_b, w_gate_b = first_w[0][None], first_w[1][None]
        dft[seq] = (cos_b, sin_b, cos_b.T, sin_b.T)

    rope_cos, rope_sin = _rope_tables(s_lat)
    cache_k2 = cache_k.reshape(n_lat, depth, past, KV_WIDTH)
    cache_v2 = cache_v.reshape(n_lat, depth, past, KV_WIDTH)
    b_gate3 = b_gate.reshape(depth, 1, -1)
    skip3 = hyena_skip.reshape(depth, 1, hw)

    groups = ((0, m_ctx), (m_ctx, m_lat))

    def res_stage(x_parts, terms, split_out, **kw):
        tm = norm_tile(terms, kw["write_x"])
        common = (terms, norm_tab, mod_tab, seq_of_tile(tm), tm)
        if len(x_parts) == 1 and not split_out:
            return _resnorm(x_parts[0][0], *common, rows=(0, m), **kw)
        outs, per_group = None, []
        for gi, (g0, n_rows) in enumerate(groups):
            xa, xb = x_parts[gi] if len(x_parts) > 1 else x_parts[0]
            if split_out:
                per_group.append(_resnorm(xa, *common, rows=(g0, n_rows), x_base=xb, out_base=g0, **kw))
            else:
                outs = _resnorm(xa, *common, rows=(g0, n_rows), x_base=xb, out_rows=m, into=outs, **kw)
        return per_group if split_out else outs

    x_parts = [(x_prompt.reshape(m_ctx, d), 0), (x_sample.reshape(m_lat, d), m_ctx)]
    (h,) = res_stage(x_parts, [], False, write_x=False, h_norm=0, h_scale=(0, 1), h_shift=(0, 0))
    new_kv = None
    for l in range(depth):
        proj, (w_branch_b,) = _matmul(h, w_in_b, 0, F32, 1024, 1024, [(w_branch, l)])
        gates, (w_o_b,) = _gates(h, w_gate_b, b_gate3, 0, l, 1024, 1024, [(w_o, l)])

        o_attn, new_kv = _ctx_attention(proj, attn_sink, l, n_ctx, s_ctx, m, depth, new_kv)
        o_attn = _lat_attention(proj, cache_k2, cache_v2, attn_sink, rope_cos, rope_sin, l, m_ctx, n_lat, s_lat, o_attn)
        o_conv = _short_conv(proj, conv_w, l, col_conv, 0, n_ctx, s_ctx, 1024, m)
        o_conv = _short_conv(proj, conv_w, l, col_conv, m_ctx, n_lat, s_lat, 512, m, o_conv)
        o_hy = _hyena(proj, hyena_conv_w, skip3, *spectra[s_ctx], dft[s_ctx], l, col_hy, 0, n_ctx, s_ctx, 1024, 512, m)
        o_hy = _hyena(proj, hyena_conv_w, skip3, *spectra[s_lat], dft[s_lat], l, col_hy, m_ctx, n_lat, s_lat, 512, 256,
                      m, o_hy)

        merged, (w_ffn_in_b,) = _merge(o_attn, o_conv, o_hy, w_branch_b[None], gates, 0, 1024, 512, [(w_ffn_in, l)])
        mix, _ = _matmul(merged, w_o_b[None], 0, BF16, 1024, 1024)
        mixer_term = (mix, l * 4 + 1, (l, 2))
        (h,) = res_stage(x_parts, [mixer_term], False, write_x=False,
                         h_norm=l * 4 + 2, h_scale=(l, 4), h_shift=(l, 3))
        next_casts = [(w_in, l + 1), (w_gate, l + 1)] if l + 1 < depth else []
        hidden, (w_ffn_out_b, *next_w) = _ffn_in(h, w_ffn_in_b[None], 0, 1024, 2, [(w_ffn_out, l)] + next_casts)
        if next_w:
            w_in_b, w_gate_b = next_w[0][None], next_w[1][None]
        f, _ = _matmul(hidden, w_ffn_out_b[None], 0, BF16, 512, 512)
        layer_terms = [mixer_term, (f, l * 4 + 3, (l, 5))]
        if l + 1 < depth:
            x, h = res_stage(x_parts, layer_terms, False, write_x=True,
                             h_norm=(l + 1) * 4, h_scale=(l + 1, 1), h_shift=(l + 1, 0))
            x_parts = [(x, 0)]
        else:
            (y_prompt,), (y_sample,) = res_stage(x_parts, layer_terms, True, write_x=True)

    return (y_prompt.reshape(n_ctx, s_ctx, d), y_sample.reshape(n_lat, s_lat, d),
            new_kv[0].reshape(n_ctx, depth, s_ctx, N_KV_HEADS, HEAD_DIM),
            new_kv[1].reshape(n_ctx, depth, s_ctx, N_KV_HEADS, HEAD_DIM))
```
